```python
import jax, jax.numpy as jnp
from jax import lax
import numpy as np

D_MODEL = 4096
BATCH = 1
SEQ = 16384
DEPTH = 2
DEC_BATCH = 16
DEC_SEQ = 16
PAST_LEN = 4096

CHUNK = 64
N_MIXERS = 2
N_CONV_LAYERS = (DEPTH + 1) // 2
N_SB_LAYERS = DEPTH // 2
CONV_W = 3
N_HEADS = 32
HEAD_DIM = D_MODEL // N_HEADS
SB_SCALE = HEAD_DIM ** -0.5
Q_BLOCK = 128
K_BLOCK = 128
N_Q_GROUPS = 8
D_FF = ((8 * D_MODEL + 3 * 256 - 1) // (3 * 256)) * 256
EPS = 1e-6

kernel_name = "streaming_conv_stickbreaking_hybrid_step"


def rms_norm(x, g):
    xf = x.astype(jnp.float32)
    y = xf * lax.rsqrt(jnp.mean(xf * xf, axis=-1, keepdims=True) + EPS)
    return (y * g.astype(jnp.float32)).astype(x.dtype)


def swiglu(h, w_gate, w_up, w_down):
    return (jax.nn.silu(h @ w_gate) * (h @ w_up)) @ w_down


def short_conv_mixer(h, conv_state, w_in, w_conv, w_out):
    b_gate, c_gate, xin = jnp.split(h @ w_in, 3, axis=-1)
    u = c_gate * xin
    u_ext = jnp.concatenate([conv_state.astype(u.dtype), u], axis=1)
    t = h.shape[1]
    conv = u_ext[:, 0:t] * w_conv[0]
    for k in range(1, CONV_W):
        conv = conv + u_ext[:, k:k + t] * w_conv[k]
    y = (b_gate * conv) @ w_out
    new_state = u_ext[:, u_ext.shape[1] - (CONV_W - 1):]
    return y, new_state


def sb_project(h, w_qkv, g_q, g_k):
    b, t, _ = h.shape
    q, k, v = jnp.split(h @ w_qkv, 3, axis=-1)
    q = rms_norm(q.reshape(b, t, N_HEADS, HEAD_DIM), g_q)
    k = rms_norm(k.reshape(b, t, N_HEADS, HEAD_DIM), g_k)
    v = v.reshape(b, t, N_HEADS, HEAD_DIM)
    return q, k, v


def stick_breaking_block(q, k, v, q_pos, k_pos):
    b, tq, _, _ = q.shape
    tk = k.shape[1]
    nkb = tk // K_BLOCK
    z = jnp.einsum("bqhd,bkhd->bhqk", q.astype(jnp.float32), k.astype(jnp.float32)) * SB_SCALE
    causal = k_pos[None, :] < q_pos[:, None]
    log_stay = jnp.where(causal, -jax.nn.softplus(z), 0.0)
    ls = log_stay.reshape(b, N_HEADS, tq, nkb, K_BLOCK)
    idx = jnp.arange(K_BLOCK)
    upper = (idx[:, None] >= idx[None, :]).astype(jnp.float32)
    within = jnp.einsum("bhqnj,js->bhqns", ls, upper)
    blk = jnp.arange(nkb)
    later = (blk[:, None] > blk[None, :]).astype(jnp.float32)
    after = jnp.einsum("bhqm,mn->bhqn", ls.sum(axis=-1), later)
    suffix = (within + after[..., None]).reshape(b, N_HEADS, tq, tk)
    a = jnp.where(causal, jnp.exp(z + suffix), 0.0)
    o = jnp.einsum("bhqk,bkhd->bqhd", a, v.astype(jnp.float32))
    return o.astype(v.dtype)


def sb_prompt(q, k, v):
    b, s, _, _ = q.shape
    nb = s // Q_BLOCK
    n_groups = min(N_Q_GROUPS, nb)
    bounds = [(g * nb) // n_groups for g in range(n_groups + 1)]
    outs = []
    for g in range(n_groups):
        b0, b1 = bounds[g], bounds[g + 1]
        key_len = b1 * Q_BLOCK
        kg, vg = k[:, :key_len], v[:, :key_len]
        k_pos = jnp.arange(key_len, dtype=jnp.int32)
        qg = q[:, b0 * Q_BLOCK:b1 * Q_BLOCK].reshape(b, b1 - b0, Q_BLOCK, N_HEADS, HEAD_DIM).transpose(1, 0, 2, 3, 4)

        def body(args, kg=kg, vg=vg, k_pos=k_pos):
            qi, bi = args
            q_pos = bi * Q_BLOCK + jnp.arange(Q_BLOCK, dtype=jnp.int32)
            return stick_breaking_block(qi, kg, vg, q_pos, k_pos)

        o = lax.map(body, (qg, jnp.arange(b0, b1, dtype=jnp.int32)))
        outs.append(o.transpose(1, 0, 2, 3, 4).reshape(b, (b1 - b0) * Q_BLOCK, N_HEADS, HEAD_DIM))
    return jnp.concatenate(outs, axis=1)


def sb_sample(q, k, v, cache_k, cache_v):
    past = cache_k.shape[1]
    t = q.shape[1]
    total = past + t
    pad = (-total) % K_BLOCK
    bq = q.shape[0]
    zeros = jnp.zeros((bq, pad, N_HEADS, HEAD_DIM), k.dtype)
    k_all = jnp.concatenate([cache_k.astype(k.dtype), k, zeros], axis=1)
    v_all = jnp.concatenate([cache_v.astype(v.dtype), v, zeros], axis=1)
    q_pos = past + jnp.arange(t, dtype=jnp.int32)
    k_pos = jnp.arange(total + pad, dtype=jnp.int32)
    return stick_breaking_block(q, k_all, v_all, q_pos, k_pos)


def setup_inputs(seed: int = 0) -> dict:
    key = jax.random.key(seed)
    ks = jax.random.split(key, 17)
    f32 = jnp.float32

    def w(k, shape, fan_in):
        return jax.random.normal(k, shape, f32) * (fan_in ** -0.5)

    def gain(k, shape):
        return 1.0 + 0.02 * jax.random.normal(k, shape, f32)

    return {
        "x_prompt": jax.random.normal(ks[0], (BATCH, SEQ, D_MODEL), f32),
        "x_sample": jax.random.normal(ks[1], (DEC_BATCH, DEC_SEQ, D_MODEL), f32),
        "state_conv": jax.random.normal(ks[2], (N_CONV_LAYERS, DEC_BATCH, CONV_W - 1, D_MODEL), f32),
        "cache_k": jax.random.normal(ks[3], (N_SB_LAYERS, DEC_BATCH, PAST_LEN, N_HEADS, HEAD_DIM), f32),
        "cache_v": jax.random.normal(ks[4], (N_SB_LAYERS, DEC_BATCH, PAST_LEN, N_HEADS, HEAD_DIM), f32),
        "g_mix": gain(ks[5], (DEPTH, D_MODEL)),
        "g_ffn": gain(ks[6], (DEPTH, D_MODEL)),
        "w_conv_in": w(ks[7], (N_CONV_LAYERS, D_MODEL, 3 * D_MODEL), D_MODEL),
        "w_conv": w(ks[8], (N_CONV_LAYERS, CONV_W, D_MODEL), CONV_W),
        "w_conv_out": w(ks[9], (N_CONV_LAYERS, D_MODEL, D_MODEL), D_MODEL),
        "w_qkv": w(ks[10], (N_SB_LAYERS, D_MODEL, 3 * N_HEADS * HEAD_DIM), D_MODEL),
        "g_q": gain(ks[11], (N_SB_LAYERS, HEAD_DIM)),
        "g_k": gain(ks[12], (N_SB_LAYERS, HEAD_DIM)),
        "w_o": w(ks[13], (N_SB_LAYERS, N_HEADS * HEAD_DIM, D_MODEL), N_HEADS * HEAD_DIM),
        "w_gate": w(ks[14], (DEPTH, D_MODEL, D_FF), D_MODEL),
        "w_up": w(ks[15], (DEPTH, D_MODEL, D_FF), D_MODEL),
        "w_down": w(ks[16], (DEPTH, D_FF, D_MODEL), D_FF),
    }


def reference(x_prompt, x_sample, state_conv, cache_k, cache_v, g_mix, g_ffn,
              w_conv_in, w_conv, w_conv_out, w_qkv, g_q, g_k, w_o,
              w_gate, w_up, w_down):
    xp, xs = x_prompt, x_sample
    conv_p, conv_s, k_p, v_p, k_s, v_s = [], [], [], [], [], []
    for i in range(DEPTH):
        j = i // N_MIXERS
        hp = rms_norm(xp, g_mix[i])
        hs = rms_norm(xs, g_mix[i])
        if i % N_MIXERS == 0:
            fresh = jnp.zeros((xp.shape[0], CONV_W - 1, D_MODEL), xp.dtype)
            yp, sp = short_conv_mixer(hp, fresh, w_conv_in[j], w_conv[j], w_conv_out[j])
            ys, ss = short_conv_mixer(hs, state_conv[j], w_conv_in[j], w_conv[j], w_conv_out[j])
            conv_p.append(sp)
            conv_s.append(ss)
        else:
            qp, kp, vp = sb_project(hp, w_qkv[j], g_q[j], g_k[j])
            qs, kss, vss = sb_project(hs, w_qkv[j], g_q[j], g_k[j])
            op = sb_prompt(qp, kp, vp)
            os_ = sb_sample(qs, kss, vss, cache_k[j], cache_v[j])
            yp = op.reshape(op.shape[0], op.shape[1], N_HEADS * HEAD_DIM) @ w_o[j]
            ys = os_.reshape(os_.shape[0], os_.shape[1], N_HEADS * HEAD_DIM) @ w_o[j]
            k_p.append(kp)
            v_p.append(vp)
            k_s.append(kss)
            v_s.append(vss)
        xp = xp + yp
        xs = xs + ys
        xp = xp + swiglu(rms_norm(xp, g_ffn[i]), w_gate[i], w_up[i], w_down[i])
        xs = xs + swiglu(rms_norm(xs, g_ffn[i]), w_gate[i], w_up[i], w_down[i])
    new_conv_prompt = jnp.stack(conv_p)
    new_conv_sample = jnp.stack(conv_s)
    new_k_prompt = jnp.stack(k_p)
    new_v_prompt = jnp.stack(v_p)
    new_k_sample = jnp.stack(k_s)
    new_v_sample = jnp.stack(v_s)
    return (xp, xs, new_conv_prompt, new_conv_sample, new_k_prompt, new_v_prompt, new_k_sample, new_v_sample)
```

```python
import functools

import jax
import jax.numpy as jnp
from jax import lax
from jax.experimental import pallas as pl
from jax.experimental.pallas import tpu as pltpu

EPS = 1e-6
HEAD_DIM = 128
CONV_W = 3
N_MIXERS = 2
SB_SCALE = HEAD_DIM ** -0.5

_BF16 = jnp.bfloat16
_F32 = jnp.float32
_SUBLANES = 8
_V7X_VMEM_BYTES = 64 * 1024 * 1024
_VMEM_CAP_BYTES = _V7X_VMEM_BYTES - 6 * 1024 * 1024


def _nbytes(shape, dtype):
    n = jnp.dtype(dtype).itemsize
    for s in shape:
        n *= s
    return n


def _params(n_grid, block_bytes, extra_bytes=0):
    est = 2 * block_bytes + extra_bytes + 4 * 1024 * 1024
    return pltpu.CompilerParams(
        dimension_semantics=("arbitrary",) * n_grid,
        vmem_limit_bytes=int(min(max(est, 16 * 1024 * 1024), _VMEM_CAP_BYTES)),
    )


def _tile(dim, want):
    t = min(dim, want)
    while dim % t:
        t -= 1
    return t


def _dot(a, b):
    return jnp.dot(a, b, preferred_element_type=_F32)


def _rmsnorm_kernel(x_ref, g_ref, o_ref):
    x = x_ref[...]
    ms = jnp.mean(x * x, axis=-1, keepdims=True)
    o_ref[...] = (x * lax.rsqrt(ms + EPS) * g_ref[...]).astype(o_ref.dtype)


def _rmsnorm(x, g):
    m, d = x.shape
    tm = _tile(m, 256)
    blocks = _nbytes((tm, d), _F32) + _nbytes((tm, d), _BF16)
    return pl.pallas_call(
        _rmsnorm_kernel,
        grid=(m // tm,),
        in_specs=[pl.BlockSpec((tm, d), lambda i: (i, 0)), pl.BlockSpec((1, d), lambda i: (0, 0))],
        out_specs=pl.BlockSpec((tm, d), lambda i: (i, 0)),
        out_shape=jax.ShapeDtypeStruct((m, d), _BF16),
        compiler_params=_params(1, blocks, _nbytes((tm, d), _F32)),
        name="rmsnorm",
    )(x, g.reshape(1, d))


def _conv_in_kernel(h_ref, wb_ref, wc_ref, wx_ref, wconv_ref, st_ref, g_ref, tail_ref, u_scr,
                    *, seg_len, tail_rows):
    i = pl.program_id(1)
    hb = h_ref[...]
    b = _dot(hb, wb_ref[...])
    c = _dot(hb, wc_ref[...])
    xin = _dot(hb, wx_ref[...])
    u = c * xin
    tm = u.shape[0]
    u_scr[_SUBLANES:_SUBLANES + tm, :] = u
    if seg_len is None:
        @pl.when(i == 0)
        def _():
            u_scr[0:_SUBLANES - 2, :] = jnp.zeros((_SUBLANES - 2, u.shape[1]), _F32)
            u_scr[_SUBLANES - 2:_SUBLANES, :] = st_ref[...]
        u1 = u_scr[_SUBLANES - 1:_SUBLANES - 1 + tm, :]
        u2 = u_scr[_SUBLANES - 2:_SUBLANES - 2 + tm, :]
    else:
        u_scr[0:_SUBLANES, :] = jnp.zeros((_SUBLANES, u.shape[1]), _F32)
        r = lax.broadcasted_iota(jnp.int32, u.shape, 0) % seg_len
        u1 = jnp.where(r == 0, st_ref[0], u_scr[_SUBLANES - 1:_SUBLANES - 1 + tm, :])
        u2 = jnp.where(r < 2, st_ref[1], u_scr[_SUBLANES - 2:_SUBLANES - 2 + tm, :])
    conv = u2 * wconv_ref[0:1, :] + u1 * wconv_ref[1:2, :] + u * wconv_ref[2:3, :]
    g_ref[...] = (b * conv).astype(g_ref.dtype)
    tail_ref[...] = u[tm - tail_rows:, :]
    if seg_len is None:
        u_scr[0:_SUBLANES, :] = u[tm - _SUBLANES:, :]


def _conv_in(h, w_in, w_conv, state, seg_len):
    m, d = h.shape
    tn = _tile(d, 512)
    nj = d // tn
    if seg_len is None:
        tm = _tile(m, 512)
        tail_rows, tail_shape = _SUBLANES, (_SUBLANES, d)
        st_spec = pl.BlockSpec((2, tn), lambda j, i: (0, j))
        tail_spec = pl.BlockSpec((_SUBLANES, tn), lambda j, i: (0, j))
        st_bytes = _nbytes((2, tn), _F32)
    else:
        tm = m
        tail_rows, tail_shape = m, (m, d)
        st_spec = pl.BlockSpec((2, tm, tn), lambda j, i: (0, 0, j))
        tail_spec = pl.BlockSpec((tm, tn), lambda j, i: (0, j))
        st_bytes = _nbytes((2, tm, tn), _F32)
    blocks = (_nbytes((tm, d), _BF16) + 3 * _nbytes((d, tn), _BF16) + st_bytes
              + _nbytes((tm, tn), _BF16) + _nbytes((tail_rows, tn), _F32))
    scratch = _nbytes((tm + _SUBLANES, tn), _F32)
    return pl.pallas_call(
        functools.partial(_conv_in_kernel, seg_len=seg_len, tail_rows=tail_rows),
        grid=(nj, m // tm),
        in_specs=[
            pl.BlockSpec((tm, d), lambda j, i: (i, 0)),
            pl.BlockSpec((d, tn), lambda j, i: (0, j)),
            pl.BlockSpec((d, tn), lambda j, i: (0, j + nj)),
            pl.BlockSpec((d, tn), lambda j, i: (0, j + 2 * nj)),
            pl.BlockSpec((CONV_W, tn), lambda j, i: (0, j)),
            st_spec,
        ],
        out_specs=[pl.BlockSpec((tm, tn), lambda j, i: (i, j)), tail_spec],
        out_shape=[jax.ShapeDtypeStruct((m, d), _BF16), jax.ShapeDtypeStruct(tail_shape, _F32)],
        scratch_shapes=[pltpu.VMEM((tm + _SUBLANES, tn), _F32)],
        compiler_params=_params(2, blocks, scratch + 6 * _nbytes((tm, tn), _F32)),
        name="conv_in",
    )(h, w_in, w_in, w_in, w_conv, state)


def _matmul_res_kernel(a_ref, w_ref, r_ref, o_ref):
    o_ref[...] = _dot(a_ref[...], w_ref[...]) + r_ref[...]


def _matmul_res(a, w, res, tm_want, tn_want, w_resident):
    m, k = a.shape
    n = w.shape[1]
    tm, tn = _tile(m, tm_want), _tile(n, tn_want)
    if w_resident:
        grid = (n // tn, m // tm)
        a_map, w_map, o_map = (lambda j, i: (i, 0)), (lambda j, i: (0, j)), (lambda j, i: (i, j))
    else:
        grid = (m // tm, n // tn)
        a_map, w_map, o_map = (lambda i, j: (i, 0)), (lambda i, j: (0, j)), (lambda i, j: (i, j))
    blocks = _nbytes((tm, k), _BF16) + _nbytes((k, tn), _BF16) + 2 * _nbytes((tm, tn), _F32)
    return pl.pallas_call(
        _matmul_res_kernel,
        grid=grid,
        in_specs=[pl.BlockSpec((tm, k), a_map), pl.BlockSpec((k, tn), w_map), pl.BlockSpec((tm, tn), o_map)],
        out_specs=pl.BlockSpec((tm, tn), o_map),
        out_shape=jax.ShapeDtypeStruct((m, n), _F32),
        compiler_params=_params(2, blocks, 2 * _nbytes((tm, tn), _F32)),
        name="matmul_res",
    )(a, w, res)


def _ffn_up_kernel(h_ref, wg_ref, wu_ref, o_ref):
    hb = h_ref[...]
    gate = _dot(hb, wg_ref[...])
    up = _dot(hb, wu_ref[...])
    o_ref[...] = (gate * jax.nn.sigmoid(gate) * up).astype(o_ref.dtype)


def _ffn_up(h, wg, wu):
    m, d = h.shape
    f = wg.shape[1]
    tm, tn = _tile(m, 2048), _tile(f, 256)
    blocks = _nbytes((tm, d), _BF16) + 2 * _nbytes((d, tn), _BF16) + _nbytes((tm, tn), _BF16)
    return pl.pallas_call(
        _ffn_up_kernel,
        grid=(m // tm, f // tn),
        in_specs=[
            pl.BlockSpec((tm, d), lambda i, j: (i, 0)),
            pl.BlockSpec((d, tn), lambda i, j: (0, j)),
            pl.BlockSpec((d, tn), lambda i, j: (0, j)),
        ],
        out_specs=pl.BlockSpec((tm, tn), lambda i, j: (i, j)),
        out_shape=jax.ShapeDtypeStruct((m, f), _BF16),
        compiler_params=_params(2, blocks, 4 * _nbytes((tm, tn), _F32)),
        name="ffn_up",
    )(h, wg, wu)


def _qkv_kernel(h_ref, wq_ref, wk_ref, wv_ref, gq_ref, gk_ref,
                qh_ref, kh_ref, vh_ref, vt_ref, kf_ref, vf_ref, *, tk):
    hb = h_ref[...]
    q = _dot(hb, wq_ref[...])
    k = _dot(hb, wk_ref[...])
    v = _dot(hb, wv_ref[...])
    tm = q.shape[0]

    def head_norm(x, g):
        y = x * lax.rsqrt(jnp.mean(x * x, axis=-1, keepdims=True) + EPS)
        return y * g

    for hh in range(q.shape[1] // HEAD_DIM):
        sl = slice(hh * HEAD_DIM, (hh + 1) * HEAD_DIM)
        qh_ref[hh] = head_norm(q[:, sl], gq_ref[...]).astype(qh_ref.dtype)
        kn = head_norm(k[:, sl], gk_ref[...])
        kf_ref[:, sl] = kn
        kh_ref[hh] = kn.astype(kh_ref.dtype)
        vh = v[:, sl]
        vf_ref[:, sl] = vh
        vh_ref[hh] = vh.astype(vh_ref.dtype)
        vt = vh.T.astype(vt_ref.dtype)
        for cc in range(tm // tk):
            vt_ref[hh, cc] = vt[:, cc * tk:(cc + 1) * tk]


def _qkv(h, w_qkv, g_q, g_k, tk):
    m, d = h.shape
    nh = d // HEAD_DIM
    tn = _tile(d, 512)
    hpt = tn // HEAD_DIM
    nj = d // tn
    tm = _tile(m, 512)
    hm = jax.ShapeDtypeStruct((nh, m, HEAD_DIM), _BF16)
    hm_spec = pl.BlockSpec((hpt, tm, HEAD_DIM), lambda j, i: (j, i, 0))
    blocks = (_nbytes((tm, d), _BF16) + 3 * _nbytes((d, tn), _BF16)
              + 4 * _nbytes((tm, tn), _BF16) + 2 * _nbytes((tm, tn), _F32))
    return pl.pallas_call(
        functools.partial(_qkv_kernel, tk=tk),
        grid=(nj, m // tm),
        in_specs=[
            pl.BlockSpec((tm, d), lambda j, i: (i, 0)),
            pl.BlockSpec((d, tn), lambda j, i: (0, j)),
            pl.BlockSpec((d, tn), lambda j, i: (0, j + nj)),
            pl.BlockSpec((d, tn), lambda j, i: (0, j + 2 * nj)),
            pl.BlockSpec((1, HEAD_DIM), lambda j, i: (0, 0)),
            pl.BlockSpec((1, HEAD_DIM), lambda j, i: (0, 0)),
        ],
        out_specs=[
            hm_spec, hm_spec, hm_spec,
            pl.BlockSpec((hpt, tm // tk, HEAD_DIM, tk), lambda j, i: (j, i, 0, 0)),
            pl.BlockSpec((tm, tn), lambda j, i: (i, j)),
            pl.BlockSpec((tm, tn), lambda j, i: (i, j)),
        ],
        out_shape=[hm, hm, hm,
                   jax.ShapeDtypeStruct((nh, m // tk, HEAD_DIM, tk), _BF16),
                   jax.ShapeDtypeStruct((m, d), _F32), jax.ShapeDtypeStruct((m, d), _F32)],
        compiler_params=_params(2, blocks, 8 * _nbytes((tm, tn), _F32)),
        name="qkv",
    )(h, w_qkv, w_qkv, w_qkv, g_q.reshape(1, HEAD_DIM), g_k.reshape(1, HEAD_DIM))


def _sb_block(kblk, vt, q, tri, r_ref, acc_ref, masked):
    z = lax.dot_general(kblk, q, (((1,), (1,)), ((), ())), preferred_element_type=_F32) * SB_SCALE
    sp = jnp.maximum(z, 0.0) + jnp.log1p(jnp.exp(-jnp.abs(z)))
    if masked:
        causal = lax.broadcasted_iota(jnp.int32, z.shape, 0) < lax.broadcasted_iota(jnp.int32, z.shape, 1)
        sp = jnp.where(causal, sp, 0.0)
    cs = _dot(tri, sp.astype(_BF16))
    r = r_ref[0:1, :]
    a = jnp.exp(z - cs + r)
    if masked:
        a = jnp.where(causal, a, 0.0)
    acc_ref[...] += _dot(vt, a.astype(_BF16))
    r_ref[0:1, :] = r - cs[0:1, :]


def _attn_prompt_kernel(q_ref, k_ref, vt_ref, tri_ref, o_ref, r_ref, acc_ref, *, tk):
    qi = pl.program_id(1)
    q = q_ref[...]
    tri = tri_ref[...]
    r_ref[...] = jnp.zeros_like(r_ref)
    acc_ref[...] = jnp.zeros_like(acc_ref)
    _sb_block(k_ref[pl.ds(pl.multiple_of(qi * tk, tk), tk), :], vt_ref[qi], q, tri, r_ref, acc_ref, True)

    def body(t, carry):
        kb = qi - 1 - t
        _sb_block(k_ref[pl.ds(pl.multiple_of(kb * tk, tk), tk), :], vt_ref[kb], q, tri, r_ref, acc_ref, False)
        return carry

    lax.fori_loop(0, qi, body, 0)
    o_ref[...] = acc_ref[...].T.astype(o_ref.dtype)


def _tri(tk):
    idx = jnp.arange(tk)
    return (idx[None, :] >= idx[:, None]).astype(_BF16)


def _attn_prompt(q_hm, k_hm, vt, tk):
    nh, m, hd = q_hm.shape
    blocks = (_nbytes((tk, hd), _BF16) + 2 * _nbytes((m, hd), _BF16) + _nbytes((tk, tk), _BF16)
              + _nbytes((tk, hd), _BF16))
    return pl.pallas_call(
        functools.partial(_attn_prompt_kernel, tk=tk),
        grid=(nh, m // tk),
        in_specs=[
            pl.BlockSpec((None, tk, hd), lambda h, i: (h, i, 0)),
            pl.BlockSpec((None, m, hd), lambda h, i: (h, 0, 0)),
            pl.BlockSpec((None, m // tk, hd, tk), lambda h, i: (h, 0, 0, 0)),
            pl.BlockSpec((tk, tk), lambda h, i: (0, 0)),
        ],
        out_specs=pl.BlockSpec((tk, hd), lambda h, i: (i, h)),
        out_shape=jax.ShapeDtypeStruct((m, nh * hd), _BF16),
        scratch_shapes=[pltpu.VMEM((_SUBLANES, tk), _F32), pltpu.VMEM((hd, tk), _F32)],
        compiler_params=_params(2, blocks, 8 * _nbytes((tk, tk), _F32)),
        name="sb_attn_prompt",
    )(q_hm, k_hm, vt, _tri(tk))


def _attn_sample_kernel(q_ref, kn_ref, vn_ref, ck_ref, cv_ref, tri_ref, o_ref, r_ref, acc_ref,
                        *, tk, tq, n_past_blocks):
    t, hd = q_ref.shape
    q = jnp.concatenate([q_ref[...], jnp.zeros((tq - t, hd), _BF16)], axis=0)
    tri = tri_ref[...]
    r_ref[...] = jnp.zeros_like(r_ref)
    acc_ref[...] = jnp.zeros_like(acc_ref)
    k_new = jnp.concatenate([kn_ref[...], jnp.zeros((tk - t, hd), _BF16)], axis=0)
    v_new = jnp.concatenate([vn_ref[...].astype(_F32), jnp.zeros((tk - t, hd), _F32)], axis=0)
    _sb_block(k_new, v_new.T.astype(_BF16), q, tri, r_ref, acc_ref, True)

    def body(i, carry):
        start = pl.multiple_of((n_past_blocks - 1 - i) * tk, tk)
        kblk = ck_ref[pl.ds(start, tk), :].astype(_BF16)
        vt = cv_ref[pl.ds(start, tk), :].T.astype(_BF16)
        _sb_block(kblk, vt, q, tri, r_ref, acc_ref, False)
        return carry

    lax.fori_loop(0, n_past_blocks, body, 0)
    o_ref[...] = acc_ref[...].T[0:t, :].astype(o_ref.dtype)


def _attn_sample(q_hm, k_hm, v_hm, cache_k, cache_v, n_streams, tk):
    nh, m, hd = q_hm.shape
    t = m // n_streams
    past = cache_k.shape[1]
    assert past % tk == 0 and t <= HEAD_DIM
    tq = HEAD_DIM
    new_spec = pl.BlockSpec((None, t, hd), lambda s, h: (h, s, 0))
    cache_spec = pl.BlockSpec((None, past, hd), lambda s, h: (s, 0, h))
    blocks = 3 * _nbytes((t, hd), _BF16) + 2 * _nbytes((past, hd), _F32) + _nbytes((tk, tk), _BF16)
    return pl.pallas_call(
        functools.partial(_attn_sample_kernel, tk=tk, tq=tq, n_past_blocks=past // tk),
        grid=(n_streams, nh),
        in_specs=[new_spec, new_spec, new_spec, cache_spec, cache_spec, pl.BlockSpec((tk, tk), lambda s, h: (0, 0))],
        out_specs=pl.BlockSpec((t, hd), lambda s, h: (s, h)),
        out_shape=jax.ShapeDtypeStruct((m, nh * hd), _BF16),
        scratch_shapes=[pltpu.VMEM((_SUBLANES, tq), _F32), pltpu.VMEM((hd, tq), _F32)],
        compiler_params=_params(2, blocks, 8 * _nbytes((tk, tq), _F32)),
        name="sb_attn_sample",
    )(q_hm, k_hm, v_hm, cache_k, cache_v, _tri(tk))


def _ffn(x, g, wg, wu, wd):
    mid = _ffn_up(_rmsnorm(x, g), wg, wu)
    return _matmul_res(mid, wd, x, 512, 256, w_resident=False)


def kernel(x_prompt, x_sample, state_conv, cache_k, cache_v, g_mix, g_ffn, w_conv_in, w_conv, w_conv_out,
           w_qkv, g_q, g_k, w_o, w_gate, w_up, w_down):
    batch, seq, d = x_prompt.shape
    dec_batch, dec_seq, _ = x_sample.shape
    assert batch == 1, "the prompt path handles one fresh stream"
    nh = d // HEAD_DIM
    depth = g_mix.shape[0]
    attn_tk = 256
    xp = x_prompt.reshape(seq, d)
    xs = x_sample.reshape(dec_batch * dec_seq, d)
    conv_p, conv_s, k_p, v_p, k_s, v_s = [], [], [], [], [], []
    for i in range(depth):
        j = i // N_MIXERS
        hp = _rmsnorm(xp, g_mix[i])
        hs = _rmsnorm(xs, g_mix[i])
        if i % N_MIXERS == 0:
            w_in = w_conv_in[j].astype(_BF16)
            w_out = w_conv_out[j].astype(_BF16)
            gp, tail_p = _conv_in(hp, w_in, w_conv[j], jnp.zeros((CONV_W - 1, d), _F32), None)
            st = state_conv[j]
            inject = jnp.zeros((2, dec_batch, dec_seq, d), _F32)
            inject = inject.at[0, :, 0].set(st[:, 1]).at[1, :, 0].set(st[:, 0]).at[1, :, 1].set(st[:, 1])
            gs, u_s = _conv_in(hs, w_in, w_conv[j], inject.reshape(2, dec_batch * dec_seq, d), dec_seq)
            conv_p.append(tail_p[_SUBLANES - (CONV_W - 1):].reshape(1, CONV_W - 1, d))
            conv_s.append(u_s.reshape(dec_batch, dec_seq, d)[:, dec_seq - (CONV_W - 1):])
            xp = _matmul_res(gp, w_out, xp, 1024, 512, w_resident=True)
            xs = _matmul_res(gs, w_out, xs, 1024, 512, w_resident=True)
        else:
            wqkv = w_qkv[j].astype(_BF16)
            wo = w_o[j].astype(_BF16)
            qp, kp, _, vtp, kfp, vfp = _qkv(hp, wqkv, g_q[j], g_k[j], attn_tk)
            qs, ks, vs, _, kfs, vfs = _qkv(hs, wqkv, g_q[j], g_k[j], attn_tk)
            op = _attn_prompt(qp, kp, vtp, attn_tk)
            past = cache_k.shape[2]
            os_ = _attn_sample(qs, ks, vs, cache_k[j].reshape(dec_batch, past, d),
                               cache_v[j].reshape(dec_batch, past, d), dec_batch, attn_tk)
            k_p.append(kfp.reshape(batch, seq, nh, HEAD_DIM))
            v_p.append(vfp.reshape(batch, seq, nh, HEAD_DIM))
            k_s.append(kfs.reshape(dec_batch, dec_seq, nh, HEAD_DIM))
            v_s.append(vfs.reshape(dec_batch, dec_seq, nh, HEAD_DIM))
            xp = _matmul_res(op, wo, xp, 1024, 512, w_resident=True)
            xs = _matmul_res(os_, wo, xs, 1024, 512, w_resident=True)
        wg, wu, wd = w_gate[i].astype(_BF16), w_up[i].astype(_BF16), w_down[i].astype(_BF16)
        xp = _ffn(xp, g_ffn[i], wg, wu, wd)
        xs = _ffn(xs, g_ffn[i], wg, wu, wd)
    return (xp.reshape(batch, seq, d), xs.reshape(dec_batch, dec_seq, d),
            jnp.stack(conv_p), jnp.stack(conv_s), jnp.stack(k_p), jnp.stack(v_p), jnp.stack(k_s), jnp.stack(v_s))
```

```python
import functools

import jax
import jax.numpy as jnp
from jax import lax
from jax.experimental import pallas as pl
from jax.experimental.pallas import tpu as pltpu

EPS = 1e-6
HEAD_DIM = 128
CONV_W = 3
N_MIXERS = 2
SB_SCALE = HEAD_DIM ** -0.5
_LOG2E = 1.4426950408889634
_SIGN_BIT = -2 ** 31

_BF16 = jnp.bfloat16
_F32 = jnp.float32
_SUBLANES = 8
_V7X_VMEM_BYTES = 64 * 1024 * 1024
_VMEM_CAP_BYTES = _V7X_VMEM_BYTES - 6 * 1024 * 1024


def _nbytes(shape, dtype):
    n = jnp.dtype(dtype).itemsize
    for s in shape:
        n *= s
    return n


def _params(n_grid, block_bytes, extra_bytes=0):
    est = 2 * block_bytes + extra_bytes + 4 * 1024 * 1024
    return pltpu.CompilerParams(
        dimension_semantics=("arbitrary",) * n_grid,
        vmem_limit_bytes=int(min(max(est, 16 * 1024 * 1024), _VMEM_CAP_BYTES)),
    )


def _tile(dim, want):
    t = min(dim, want)
    while dim % t:
        t -= 1
    return t


def _dot(a, b):
    return jnp.dot(a, b, preferred_element_type=_F32)


def _rmsnorm_kernel(x_ref, g_ref, o_ref):
    x = x_ref[...]
    ms = jnp.mean(x * x, axis=-1, keepdims=True)
    o_ref[...] = (x * lax.rsqrt(ms + EPS) * g_ref[...]).astype(o_ref.dtype)


def _rmsnorm(x, g):
    m, d = x.shape
    tm = _tile(m, 256)
    blocks = _nbytes((tm, d), _F32) + _nbytes((tm, d), _BF16)
    return pl.pallas_call(
        _rmsnorm_kernel,
        grid=(m // tm,),
        in_specs=[pl.BlockSpec((tm, d), lambda i: (i, 0)), pl.BlockSpec((1, d), lambda i: (0, 0))],
        out_specs=pl.BlockSpec((tm, d), lambda i: (i, 0)),
        out_shape=jax.ShapeDtypeStruct((m, d), _BF16),
        compiler_params=_params(1, blocks, _nbytes((tm, d), _F32)),
        name="rmsnorm",
    )(x, g.reshape(1, d))


def _conv_in_kernel(h_ref, wb_ref, wc_ref, wx_ref, wconv_ref, st_ref, g_ref, tail_ref, u_scr,
                    *, seg_len, tail_rows):
    i = pl.program_id(1)
    hb = h_ref[...]
    b = _dot(hb, wb_ref[...])
    c = _dot(hb, wc_ref[...])
    xin = _dot(hb, wx_ref[...])
    u = c * xin
    tm = u.shape[0]
    u_scr[_SUBLANES:_SUBLANES + tm, :] = u
    if seg_len is None:
        @pl.when(i == 0)
        def _():
            u_scr[0:_SUBLANES - 2, :] = jnp.zeros((_SUBLANES - 2, u.shape[1]), _F32)
            u_scr[_SUBLANES - 2:_SUBLANES, :] = st_ref[...]
        u1 = u_scr[_SUBLANES - 1:_SUBLANES - 1 + tm, :]
        u2 = u_scr[_SUBLANES - 2:_SUBLANES - 2 + tm, :]
    else:
        u_scr[0:_SUBLANES, :] = jnp.zeros((_SUBLANES, u.shape[1]), _F32)
        r = lax.broadcasted_iota(jnp.int32, u.shape, 0) % seg_len
        u1 = jnp.where(r == 0, st_ref[0], u_scr[_SUBLANES - 1:_SUBLANES - 1 + tm, :])
        u2 = jnp.where(r < 2, st_ref[1], u_scr[_SUBLANES - 2:_SUBLANES - 2 + tm, :])
    conv = u2 * wconv_ref[0:1, :] + u1 * wconv_ref[1:2, :] + u * wconv_ref[2:3, :]
    g_ref[...] = (b * conv).astype(g_ref.dtype)
    tail_ref[...] = u[tm - tail_rows:, :]
    if seg_len is None:
        u_scr[0:_SUBLANES, :] = u[tm - _SUBLANES:, :]


def _conv_in(h, w_in, w_conv, state, seg_len):
    m, d = h.shape
    tn = _tile(d, 512)
    nj = d // tn
    if seg_len is None:
        tm = _tile(m, 512)
        tail_rows, tail_shape = _SUBLANES, (_SUBLANES, d)
        st_spec = pl.BlockSpec((2, tn), lambda j, i: (0, j))
        tail_spec = pl.BlockSpec((_SUBLANES, tn), lambda j, i: (0, j))
        st_bytes = _nbytes((2, tn), _F32)
    else:
        tm = m
        tail_rows, tail_shape = m, (m, d)
        st_spec = pl.BlockSpec((2, tm, tn), lambda j, i: (0, 0, j))
        tail_spec = pl.BlockSpec((tm, tn), lambda j, i: (0, j))
        st_bytes = _nbytes((2, tm, tn), _F32)
    blocks = (_nbytes((tm, d), _BF16) + 3 * _nbytes((d, tn), _BF16) + st_bytes
              + _nbytes((tm, tn), _BF16) + _nbytes((tail_rows, tn), _F32))
    scratch = _nbytes((tm + _SUBLANES, tn), _F32)
    return pl.pallas_call(
        functools.partial(_conv_in_kernel, seg_len=seg_len, tail_rows=tail_rows),
        grid=(nj, m // tm),
        in_specs=[
            pl.BlockSpec((tm, d), lambda j, i: (i, 0)),
            pl.BlockSpec((d, tn), lambda j, i: (0, j)),
            pl.BlockSpec((d, tn), lambda j, i: (0, j + nj)),
            pl.BlockSpec((d, tn), lambda j, i: (0, j + 2 * nj)),
            pl.BlockSpec((CONV_W, tn), lambda j, i: (0, j)),
            st_spec,
        ],
        out_specs=[pl.BlockSpec((tm, tn), lambda j, i: (i, j)), tail_spec],
        out_shape=[jax.ShapeDtypeStruct((m, d), _BF16), jax.ShapeDtypeStruct(tail_shape, _F32)],
        scratch_shapes=[pltpu.VMEM((tm + _SUBLANES, tn), _F32)],
        compiler_params=_params(2, blocks, scratch + 6 * _nbytes((tm, tn), _F32)),
        name="conv_in",
    )(h, w_in, w_in, w_in, w_conv, state)


def _matmul_res_kernel(a_ref, w_ref, r_ref, o_ref):
    o_ref[...] = _dot(a_ref[...], w_ref[...]) + r_ref[...]


def _matmul_res(a, w, res, tm_want, tn_want, w_resident):
    m, k = a.shape
    n = w.shape[1]
    tm, tn = _tile(m, tm_want), _tile(n, tn_want)
    if w_resident:
        grid = (n // tn, m // tm)
        a_map, w_map, o_map = (lambda j, i: (i, 0)), (lambda j, i: (0, j)), (lambda j, i: (i, j))
    else:
        grid = (m // tm, n // tn)
        a_map, w_map, o_map = (lambda i, j: (i, 0)), (lambda i, j: (0, j)), (lambda i, j: (i, j))
    blocks = _nbytes((tm, k), _BF16) + _nbytes((k, tn), _BF16) + 2 * _nbytes((tm, tn), _F32)
    return pl.pallas_call(
        _matmul_res_kernel,
        grid=grid,
        in_specs=[pl.BlockSpec((tm, k), a_map), pl.BlockSpec((k, tn), w_map), pl.BlockSpec((tm, tn), o_map)],
        out_specs=pl.BlockSpec((tm, tn), o_map),
        out_shape=jax.ShapeDtypeStruct((m, n), _F32),
        compiler_params=_params(2, blocks, 2 * _nbytes((tm, tn), _F32)),
        name="matmul_res",
    )(a, w, res)


def _ffn_up_kernel(h_ref, wg_ref, wu_ref, o_ref):
    hb = h_ref[...]
    gate = _dot(hb, wg_ref[...])
    up = _dot(hb, wu_ref[...])
    o_ref[...] = (gate * jax.nn.sigmoid(gate) * up).astype(o_ref.dtype)


def _ffn_up(h, wg, wu):
    m, d = h.shape
    f = wg.shape[1]
    tm, tn = _tile(m, 2048), _tile(f, 256)
    blocks = _nbytes((tm, d), _BF16) + 2 * _nbytes((d, tn), _BF16) + _nbytes((tm, tn), _BF16)
    return pl.pallas_call(
        _ffn_up_kernel,
        grid=(m // tm, f // tn),
        in_specs=[
            pl.BlockSpec((tm, d), lambda i, j: (i, 0)),
            pl.BlockSpec((d, tn), lambda i, j: (0, j)),
            pl.BlockSpec((d, tn), lambda i, j: (0, j)),
        ],
        out_specs=pl.BlockSpec((tm, tn), lambda i, j: (i, j)),
        out_shape=jax.ShapeDtypeStruct((m, f), _BF16),
        compiler_params=_params(2, blocks, 4 * _nbytes((tm, tn), _F32)),
        name="ffn_up",
    )(h, wg, wu)


def _qkv_kernel(h_ref, wq_ref, wk_ref, wv_ref, gq_ref, gk_ref, qh_ref, kh_ref, vx_ref, kf_ref, vf_ref,
                *, key_chunk, q_scale):
    hb = h_ref[...]
    q = _dot(hb, wq_ref[...])
    k = _dot(hb, wk_ref[...])
    v = _dot(hb, wv_ref[...])
    tm = q.shape[0]

    def head_norm(x, g):
        y = x * lax.rsqrt(jnp.mean(x * x, axis=-1, keepdims=True) + EPS)
        return y * g

    for hh in range(q.shape[1] // HEAD_DIM):
        sl = slice(hh * HEAD_DIM, (hh + 1) * HEAD_DIM)
        qh_ref[hh] = (head_norm(q[:, sl], gq_ref[...]) * q_scale).astype(qh_ref.dtype)
        kn = head_norm(k[:, sl], gk_ref[...])
        kf_ref[:, sl] = kn
        kh_ref[hh] = kn.astype(kh_ref.dtype)
        vh = v[:, sl]
        vf_ref[:, sl] = vh
        if key_chunk is None:
            vx_ref[hh] = vh.astype(vx_ref.dtype)
        else:
            vt = vh.T.astype(vx_ref.dtype)
            for cc in range(tm // key_chunk):
                vx_ref[hh, cc] = vt[:, cc * key_chunk:(cc + 1) * key_chunk]


def _qkv(h, w_qkv, g_q, g_k, key_chunk):
    m, d = h.shape
    nh = d // HEAD_DIM
    tn = _tile(d, 512)
    hpt = tn // HEAD_DIM
    nj = d // tn
    tm = _tile(m, 512)
    hm = jax.ShapeDtypeStruct((nh, m, HEAD_DIM), _BF16)
    hm_spec = pl.BlockSpec((hpt, tm, HEAD_DIM), lambda j, i: (j, i, 0))
    if key_chunk is None:
        vx, vx_spec = hm, hm_spec
    else:
        vx = jax.ShapeDtypeStruct((nh, m // key_chunk, HEAD_DIM, key_chunk), _BF16)
        vx_spec = pl.BlockSpec((hpt, tm // key_chunk, HEAD_DIM, key_chunk), lambda j, i: (j, i, 0, 0))
    blocks = (_nbytes((tm, d), _BF16) + 3 * _nbytes((d, tn), _BF16)
              + 3 * _nbytes((tm, tn), _BF16) + 2 * _nbytes((tm, tn), _F32))
    return pl.pallas_call(
        functools.partial(_qkv_kernel, key_chunk=key_chunk, q_scale=SB_SCALE * _LOG2E),
        grid=(nj, m // tm),
        in_specs=[
            pl.BlockSpec((tm, d), lambda j, i: (i, 0)),
            pl.BlockSpec((d, tn), lambda j, i: (0, j)),
            pl.BlockSpec((d, tn), lambda j, i: (0, j + nj)),
            pl.BlockSpec((d, tn), lambda j, i: (0, j + 2 * nj)),
            pl.BlockSpec((1, HEAD_DIM), lambda j, i: (0, 0)),
            pl.BlockSpec((1, HEAD_DIM), lambda j, i: (0, 0)),
        ],
        out_specs=[hm_spec, hm_spec, vx_spec,
                   pl.BlockSpec((tm, tn), lambda j, i: (i, j)), pl.BlockSpec((tm, tn), lambda j, i: (i, j))],
        out_shape=[hm, hm, vx, jax.ShapeDtypeStruct((m, d), _F32), jax.ShapeDtypeStruct((m, d), _F32)],
        compiler_params=_params(2, blocks, 8 * _nbytes((tm, tn), _F32)),
        name="qkv",
    )(h, w_qkv, w_qkv, w_qkv, g_q.reshape(1, HEAD_DIM), g_k.reshape(1, HEAD_DIM))


def _sb_logits(kspan, q, tri, r_ref, masks):
    tk = tri.shape[0]
    z = lax.dot_general(kspan, q, (((1,), (1,)), ((), ())), preferred_element_type=_F32)
    neg_abs = lax.bitcast_convert_type(lax.bitcast_convert_type(z, jnp.int32) | _SIGN_BIT, _F32)
    sp = jnp.maximum(z, 0.0) + jnp.log(1.0 + jnp.exp2(neg_abs)) * _LOG2E
    r = r_ref[0:1, :]
    w_blocks = [None] * len(masks)
    for b in reversed(range(len(masks))):
        zb, spb = z[b * tk:(b + 1) * tk], sp[b * tk:(b + 1) * tk]
        if masks[b]:
            causal = lax.broadcasted_iota(jnp.int32, zb.shape, 0) < lax.broadcasted_iota(jnp.int32, zb.shape, 1)
            spb = jnp.where(causal, spb, 0.0)
        cs = _dot(tri, spb.astype(_BF16))
        wb = zb - cs + r
        w_blocks[b] = jnp.where(causal, wb, -jnp.inf) if masks[b] else wb
        r = r - cs[0:1, :]
    r_ref[0:1, :] = r
    return w_blocks[0] if len(masks) == 1 else jnp.concatenate(w_blocks, axis=0)


def _sb_accumulate(w, vt, acc_ref):
    acc_ref[...] += _dot(vt, jnp.exp2(w).astype(_BF16))


def _attn_prompt_kernel(q_ref, k_ref, vt_ref, tri_ref, o_ref, r_ref, acc_ref, w_ref, *, tk, heads, nsub):
    qi = pl.program_id(1)
    tri = tri_ref[...]
    chunk = nsub * tk
    chains = [(h, c) for h in range(heads) for c in range(nsub)]
    r_ref[...] = jnp.zeros_like(r_ref)
    acc_ref[...] = jnp.zeros_like(acc_ref)
    qs = {(h, c): q_ref[h, c * tk:(c + 1) * tk, :] for h, c in chains}

    dstart = pl.multiple_of(qi * chunk, chunk)
    for n, (h, c) in enumerate(chains):
        w = _sb_logits(k_ref[h, pl.ds(dstart, (c + 1) * tk), :], qs[h, c], tri, r_ref.at[n],
                       (False,) * c + (True,))
        if c + 1 < nsub:
            w = jnp.concatenate([w, jnp.full(((nsub - c - 1) * tk, tk), -jnp.inf, _F32)], axis=0)
        w_ref[n] = w

    def logits(kc):
        start = pl.multiple_of(kc * chunk, chunk)
        for n, (h, c) in enumerate(chains):
            w_ref[n] = _sb_logits(k_ref[h, pl.ds(start, chunk), :], qs[h, c], tri, r_ref.at[n], (False,) * nsub)

    def accumulate(kc):
        for n, (h, c) in enumerate(chains):
            _sb_accumulate(w_ref[n], vt_ref[h, kc], acc_ref.at[n])

    def body(t, carry):
        accumulate(qi - t)
        logits(qi - 1 - t)
        return carry

    lax.fori_loop(0, qi, body, 0)
    accumulate(0)
    for n, (h, c) in enumerate(chains):
        o_ref[c * tk:(c + 1) * tk, h * HEAD_DIM:(h + 1) * HEAD_DIM] = acc_ref[n].T.astype(o_ref.dtype)


def _tri(tk):
    idx = jnp.arange(tk)
    return (idx[None, :] >= idx[:, None]).astype(_BF16)


def _attn_prompt(q_hm, k_hm, vt, tk, heads):
    nh, m, hd = q_hm.shape
    chunk = vt.shape[3]
    nsub = chunk // tk
    resident = dict(pipeline_mode=pl.Buffered(1))
    blocks = (2 * _nbytes((heads, chunk, hd), _BF16) + _nbytes((heads, m, hd), _BF16)
              + _nbytes((tk, tk), _BF16))
    return pl.pallas_call(
        functools.partial(_attn_prompt_kernel, tk=tk, heads=heads, nsub=nsub),
        grid=(nh // heads, m // chunk),
        in_specs=[
            pl.BlockSpec((heads, chunk, hd), lambda g, i: (g, i, 0)),
            pl.BlockSpec((heads, m, hd), lambda g, i: (g, 0, 0), **resident),
            pl.BlockSpec((heads, m // chunk, hd, chunk), lambda g, i: (g, 0, 0, 0), **resident),
            pl.BlockSpec((tk, tk), lambda g, i: (0, 0)),
        ],
        out_specs=pl.BlockSpec((chunk, heads * hd), lambda g, i: (i, g)),
        out_shape=jax.ShapeDtypeStruct((m, nh * hd), _BF16),
        scratch_shapes=[pltpu.VMEM((heads * nsub, _SUBLANES, tk), _F32), pltpu.VMEM((heads * nsub, hd, tk), _F32),
                        pltpu.VMEM((heads * nsub, chunk, tk), _F32)],
        compiler_params=_params(2, blocks, 17 * heads * nsub * _nbytes((chunk, tk), _F32)),
        name="sb_attn_prompt",
    )(q_hm, k_hm, vt, _tri(tk))


def _attn_sample_kernel(q_ref, kn_ref, vn_ref, ck_ref, cv_ref, tri_ref, o_ref, r_ref, acc_ref, w_ref,
                        *, tk, tq, heads, nsub):
    t, hd = q_ref.shape[1], q_ref.shape[2]
    past = ck_ref.shape[0]
    chunk = nsub * tk
    n_chunks = past // chunk
    tri = tri_ref[...]
    r_ref[...] = jnp.zeros_like(r_ref)
    acc_ref[...] = jnp.zeros_like(acc_ref)
    qs = [jnp.concatenate([q_ref[h], jnp.zeros((tq - t, hd), _BF16)], axis=0) for h in range(heads)]
    for h in range(heads):
        k_new = jnp.concatenate([kn_ref[h], jnp.zeros((tk - t, hd), _BF16)], axis=0)
        v_new = jnp.concatenate([vn_ref[h].astype(_F32), jnp.zeros((tk - t, hd), _F32)], axis=0)
        w = _sb_logits(k_new, qs[h], tri, r_ref.at[h], (True,))
        _sb_accumulate(w, v_new.T.astype(_BF16), acc_ref.at[h])

    def logits(kc):
        start = pl.multiple_of(kc * chunk, chunk)
        for h in range(heads):
            kspan = ck_ref[pl.ds(start, chunk), h * hd:(h + 1) * hd].astype(_BF16)
            w_ref[h] = _sb_logits(kspan, qs[h], tri, r_ref.at[h], (False,) * nsub)

    def accumulate(kc):
        start = pl.multiple_of(kc * chunk, chunk)
        for h in range(heads):
            vt = cv_ref[pl.ds(start, chunk), h * hd:(h + 1) * hd].T.astype(_BF16)
            _sb_accumulate(w_ref[h], vt, acc_ref.at[h])

    logits(n_chunks - 1)

    def body(i, carry):
        accumulate(n_chunks - 1 - i)
        logits(n_chunks - 2 - i)
        return carry

    lax.fori_loop(0, n_chunks - 1, body, 0)
    accumulate(0)
    for h in range(heads):
        o_ref[:, h * hd:(h + 1) * hd] = acc_ref[h].T[0:t, :].astype(o_ref.dtype)


def _attn_sample(q_hm, k_hm, v_hm, cache_k, cache_v, layer, tk, chunk, heads):
    nh, m, hd = q_hm.shape
    n_streams, past = cache_k.shape[1], cache_k.shape[2]
    t = m // n_streams
    assert past % chunk == 0 and t <= HEAD_DIM
    tq = HEAD_DIM
    new_spec = pl.BlockSpec((heads, t, hd), lambda s, g: (g, s, 0))
    cache_spec = pl.BlockSpec((None, None, past, heads * hd), lambda s, g: (layer, s, 0, g))
    blocks = (3 * _nbytes((heads, t, hd), _BF16) + 2 * _nbytes((past, heads * hd), _F32)
              + _nbytes((tk, tk), _BF16))
    return pl.pallas_call(
        functools.partial(_attn_sample_kernel, tk=tk, tq=tq, heads=heads, nsub=chunk // tk),
        grid=(n_streams, nh // heads),
        in_specs=[new_spec, new_spec, new_spec, cache_spec, cache_spec, pl.BlockSpec((tk, tk), lambda s, g: (0, 0))],
        out_specs=pl.BlockSpec((t, heads * hd), lambda s, g: (s, g)),
        out_shape=jax.ShapeDtypeStruct((m, nh * hd), _BF16),
        scratch_shapes=[pltpu.VMEM((heads, _SUBLANES, tq), _F32), pltpu.VMEM((heads, hd, tq), _F32),
                        pltpu.VMEM((heads, chunk, tq), _F32)],
        compiler_params=_params(2, blocks, 17 * heads * _nbytes((chunk, tq), _F32)),
        name="sb_attn_sample",
    )(q_hm, k_hm, v_hm, cache_k, cache_v, _tri(tk))


def _ffn(x, g, wg, wu, wd):
    mid = _ffn_up(_rmsnorm(x, g), wg, wu)
    return _matmul_res(mid, wd, x, 512, 256, w_resident=False)


def kernel(x_prompt, x_sample, state_conv, cache_k, cache_v, g_mix, g_ffn, w_conv_in, w_conv, w_conv_out,
           w_qkv, g_q, g_k, w_o, w_gate, w_up, w_down):
    batch, seq, d = x_prompt.shape
    dec_batch, dec_seq, _ = x_sample.shape
    assert batch == 1, "the prompt path handles one fresh stream"
    nh = d // HEAD_DIM
    depth = g_mix.shape[0]
    attn_tk, attn_chunk = 256, 512
    past = cache_k.shape[2]
    cache_k = cache_k.reshape(cache_k.shape[0], dec_batch, past, d)
    cache_v = cache_v.reshape(cache_v.shape[0], dec_batch, past, d)
    xp = x_prompt.reshape(seq, d)
    xs = x_sample.reshape(dec_batch * dec_seq, d)
    conv_p, conv_s, k_p, v_p, k_s, v_s = [], [], [], [], [], []
    for i in range(depth):
        j = i // N_MIXERS
        hp = _rmsnorm(xp, g_mix[i])
        hs = _rmsnorm(xs, g_mix[i])
        if i % N_MIXERS == 0:
            w_in = w_conv_in[j].astype(_BF16)
            w_out = w_conv_out[j].astype(_BF16)
            gp, tail_p = _conv_in(hp, w_in, w_conv[j], jnp.zeros((CONV_W - 1, d), _F32), None)
            st = state_conv[j]
            inject = jnp.zeros((2, dec_batch, dec_seq, d), _F32)
            inject = inject.at[0, :, 0].set(st[:, 1]).at[1, :, 0].set(st[:, 0]).at[1, :, 1].set(st[:, 1])
            gs, u_s = _conv_in(hs, w_in, w_conv[j], inject.reshape(2, dec_batch * dec_seq, d), dec_seq)
            conv_p.append(tail_p[_SUBLANES - (CONV_W - 1):].reshape(1, CONV_W - 1, d))
            conv_s.append(u_s.reshape(dec_batch, dec_seq, d)[:, dec_seq - (CONV_W - 1):])
            xp = _matmul_res(gp, w_out, xp, 1024, 512, w_resident=True)
            xs = _matmul_res(gs, w_out, xs, 1024, 512, w_resident=True)
        else:
            wqkv = w_qkv[j].astype(_BF16)
            wo = w_o[j].astype(_BF16)
            qp, kp, vtp, kfp, vfp = _qkv(hp, wqkv, g_q[j], g_k[j], attn_chunk)
            qs, ks, vs, kfs, vfs = _qkv(hs, wqkv, g_q[j], g_k[j], None)
            op = _attn_prompt(qp, kp, vtp, attn_tk, heads=2)
            os_ = _attn_sample(qs, ks, vs, cache_k, cache_v, j, attn_tk, attn_chunk, heads=4)
            k_p.append(kfp.reshape(batch, seq, nh, HEAD_DIM))
            v_p.append(vfp.reshape(batch, seq, nh, HEAD_DIM))
            k_s.append(kfs.reshape(dec_batch, dec_seq, nh, HEAD_DIM))
            v_s.append(vfs.reshape(dec_batch, dec_seq, nh, HEAD_DIM))
            xp = _matmul_res(op, wo, xp, 1024, 512, w_resident=True)
            xs = _matmul_res(os_, wo, xs, 1024, 512, w_resident=True)
        wg, wu, wd = w_gate[i].astype(_BF16), w_up[i].astype(_BF16), w_down[i].astype(_BF16)
        xp = _ffn(xp, g_ffn[i], wg, wu, wd)
        xs = _ffn(xs, g_ffn[i], wg, wu, wd)
    return (xp.reshape(batch, seq, d), xs.reshape(dec_batch, dec_seq, d),
            jnp.stack(conv_p), jnp.stack(conv_s), jnp.stack(k_p), jnp.stack(v_p), jnp.stack(k_s), jnp.stack(v_s))
```

```python
import functools

import jax
import jax.numpy as jnp
from jax import lax
from jax.experimental import pallas as pl
from jax.experimental.pallas import tpu as pltpu

EPS = 1e-6
HEAD_DIM = 128
CONV_W = 3
N_MIXERS = 2
SB_SCALE = HEAD_DIM ** -0.5
_LOG2E = 1.4426950408889634
_SIGN_BIT = -2 ** 31

_BF16 = jnp.bfloat16
_F32 = jnp.float32
_SUBLANES = 8
_V7X_VMEM_BYTES = 64 * 1024 * 1024
_VMEM_CAP_BYTES = _V7X_VMEM_BYTES - 6 * 1024 * 1024


def _nbytes(shape, dtype):
    n = jnp.dtype(dtype).itemsize
    for s in shape:
        n *= s
    return n


def _params(n_grid, block_bytes, extra_bytes=0):
    est = 2 * block_bytes + extra_bytes + 4 * 1024 * 1024
    return pltpu.CompilerParams(
        dimension_semantics=("arbitrary",) * n_grid,
        vmem_limit_bytes=int(min(max(est, 16 * 1024 * 1024), _VMEM_CAP_BYTES)),
    )


def _tile(dim, want):
    t = min(dim, want)
    while dim % t:
        t -= 1
    return t


def _dot(a, b):
    return jnp.dot(a, b, preferred_element_type=_F32)


def _rmsnorm_kernel(x_ref, g_ref, o_ref):
    x = x_ref[...]
    ms = jnp.mean(x * x, axis=-1, keepdims=True)
    o_ref[...] = (x * lax.rsqrt(ms + EPS) * g_ref[...]).astype(o_ref.dtype)


def _rmsnorm(x, g):
    m, d = x.shape
    tm = _tile(m, 256)
    blocks = _nbytes((tm, d), _F32) + _nbytes((tm, d), _BF16)
    return pl.pallas_call(
        _rmsnorm_kernel,
        grid=(m // tm,),
        in_specs=[pl.BlockSpec((tm, d), lambda i: (i, 0)), pl.BlockSpec((1, d), lambda i: (0, 0))],
        out_specs=pl.BlockSpec((tm, d), lambda i: (i, 0)),
        out_shape=jax.ShapeDtypeStruct((m, d), _BF16),
        compiler_params=_params(1, blocks, _nbytes((tm, d), _F32)),
        name="rmsnorm",
    )(x, g.reshape(1, d))


def _conv_in_kernel(h_ref, wb_ref, wc_ref, wx_ref, wconv_ref, st_ref, g_ref, tail_ref, u_scr,
                    *, seg_len, tail_rows):
    i = pl.program_id(1)
    hb = h_ref[...]
    b = _dot(hb, wb_ref[...])
    c = _dot(hb, wc_ref[...])
    xin = _dot(hb, wx_ref[...])
    u = c * xin
    tm = u.shape[0]
    u_scr[_SUBLANES:_SUBLANES + tm, :] = u
    if seg_len is None:
        @pl.when(i == 0)
        def _():
            u_scr[0:_SUBLANES - 2, :] = jnp.zeros((_SUBLANES - 2, u.shape[1]), _F32)
            u_scr[_SUBLANES - 2:_SUBLANES, :] = st_ref[...]
        u1 = u_scr[_SUBLANES - 1:_SUBLANES - 1 + tm, :]
        u2 = u_scr[_SUBLANES - 2:_SUBLANES - 2 + tm, :]
    else:
        u_scr[0:_SUBLANES, :] = jnp.zeros((_SUBLANES, u.shape[1]), _F32)
        r = lax.broadcasted_iota(jnp.int32, u.shape, 0) % seg_len
        u1 = jnp.where(r == 0, st_ref[0], u_scr[_SUBLANES - 1:_SUBLANES - 1 + tm, :])
        u2 = jnp.where(r < 2, st_ref[1], u_scr[_SUBLANES - 2:_SUBLANES - 2 + tm, :])
    conv = u2 * wconv_ref[0:1, :] + u1 * wconv_ref[1:2, :] + u * wconv_ref[2:3, :]
    g_ref[...] = (b * conv).astype(g_ref.dtype)
    tail_ref[...] = u[tm - tail_rows:, :]
    if seg_len is None:
        u_scr[0:_SUBLANES, :] = u[tm - _SUBLANES:, :]


def _conv_in(h, w_in, w_conv, state, seg_len):
    m, d = h.shape
    tn = _tile(d, 512)
    nj = d // tn
    if seg_len is None:
        tm = _tile(m, 512)
        tail_rows, tail_shape = _SUBLANES, (_SUBLANES, d)
        st_spec = pl.BlockSpec((2, tn), lambda j, i: (0, j))
        tail_spec = pl.BlockSpec((_SUBLANES, tn), lambda j, i: (0, j))
        st_bytes = _nbytes((2, tn), _F32)
    else:
        tm = m
        tail_rows, tail_shape = m, (m, d)
        st_spec = pl.BlockSpec((2, tm, tn), lambda j, i: (0, 0, j))
        tail_spec = pl.BlockSpec((tm, tn), lambda j, i: (0, j))
        st_bytes = _nbytes((2, tm, tn), _F32)
    blocks = (_nbytes((tm, d), _BF16) + 3 * _nbytes((d, tn), _BF16) + st_bytes
              + _nbytes((tm, tn), _BF16) + _nbytes((tail_rows, tn), _F32))
    scratch = _nbytes((tm + _SUBLANES, tn), _F32)
    return pl.pallas_call(
        functools.partial(_conv_in_kernel, seg_len=seg_len, tail_rows=tail_rows),
        grid=(nj, m // tm),
        in_specs=[
            pl.BlockSpec((tm, d), lambda j, i: (i, 0)),
            pl.BlockSpec((d, tn), lambda j, i: (0, j)),
            pl.BlockSpec((d, tn), lambda j, i: (0, j + nj)),
            pl.BlockSpec((d, tn), lambda j, i: (0, j + 2 * nj)),
            pl.BlockSpec((CONV_W, tn), lambda j, i: (0, j)),
            st_spec,
        ],
        out_specs=[pl.BlockSpec((tm, tn), lambda j, i: (i, j)), tail_spec],
        out_shape=[jax.ShapeDtypeStruct((m, d), _BF16), jax.ShapeDtypeStruct(tail_shape, _F32)],
        scratch_shapes=[pltpu.VMEM((tm + _SUBLANES, tn), _F32)],
        compiler_params=_params(2, blocks, scratch + 6 * _nbytes((tm, tn), _F32)),
        name="conv_in",
    )(h, w_in, w_in, w_in, w_conv, state)


def _matmul_res_kernel(a_ref, w_ref, r_ref, o_ref):
    o_ref[...] = _dot(a_ref[...], w_ref[...]) + r_ref[...]


def _matmul_res(a, w, res, tm_want, tn_want, w_resident):
    m, k = a.shape
    n = w.shape[1]
    tm, tn = _tile(m, tm_want), _tile(n, tn_want)
    if w_resident:
        grid = (n // tn, m // tm)
        a_map, w_map, o_map = (lambda j, i: (i, 0)), (lambda j, i: (0, j)), (lambda j, i: (i, j))
    else:
        grid = (m // tm, n // tn)
        a_map, w_map, o_map = (lambda i, j: (i, 0)), (lambda i, j: (0, j)), (lambda i, j: (i, j))
    blocks = _nbytes((tm, k), _BF16) + _nbytes((k, tn), _BF16) + 2 * _nbytes((tm, tn), _F32)
    return pl.pallas_call(
        _matmul_res_kernel,
        grid=grid,
        in_specs=[pl.BlockSpec((tm, k), a_map), pl.BlockSpec((k, tn), w_map), pl.BlockSpec((tm, tn), o_map)],
        out_specs=pl.BlockSpec((tm, tn), o_map),
        out_shape=jax.ShapeDtypeStruct((m, n), _F32),
        compiler_params=_params(2, blocks, 2 * _nbytes((tm, tn), _F32)),
        name="matmul_res",
    )(a, w, res)


def _ffn_up_kernel(h_ref, wg_ref, wu_ref, o_ref):
    hb = h_ref[...]
    gate = _dot(hb, wg_ref[...])
    up = _dot(hb, wu_ref[...])
    o_ref[...] = (gate * jax.nn.sigmoid(gate) * up).astype(o_ref.dtype)


def _ffn_up(h, wg, wu):
    m, d = h.shape
    f = wg.shape[1]
    tm, tn = _tile(m, 2048), _tile(f, 256)
    blocks = _nbytes((tm, d), _BF16) + 2 * _nbytes((d, tn), _BF16) + _nbytes((tm, tn), _BF16)
    return pl.pallas_call(
        _ffn_up_kernel,
        grid=(m // tm, f // tn),
        in_specs=[
            pl.BlockSpec((tm, d), lambda i, j: (i, 0)),
            pl.BlockSpec((d, tn), lambda i, j: (0, j)),
            pl.BlockSpec((d, tn), lambda i, j: (0, j)),
        ],
        out_specs=pl.BlockSpec((tm, tn), lambda i, j: (i, j)),
        out_shape=jax.ShapeDtypeStruct((m, f), _BF16),
        compiler_params=_params(2, blocks, 4 * _nbytes((tm, tn), _F32)),
        name="ffn_up",
    )(h, wg, wu)


def _qkv_kernel(h_ref, wq_ref, wk_ref, wv_ref, gq_ref, gk_ref, qh_ref, kh_ref, vx_ref, kf_ref, vf_ref,
                *, key_chunk, q_scale):
    hb = h_ref[...]
    q = _dot(hb, wq_ref[...])
    k = _dot(hb, wk_ref[...])
    v = _dot(hb, wv_ref[...])
    tm = q.shape[0]

    def head_norm(x, g):
        y = x * lax.rsqrt(jnp.mean(x * x, axis=-1, keepdims=True) + EPS)
        return y * g

    for hh in range(q.shape[1] // HEAD_DIM):
        sl = slice(hh * HEAD_DIM, (hh + 1) * HEAD_DIM)
        qh_ref[hh] = (head_norm(q[:, sl], gq_ref[...]) * q_scale).astype(qh_ref.dtype)
        kn = head_norm(k[:, sl], gk_ref[...])
        kf_ref[:, sl] = kn
        kh_ref[hh] = kn.astype(kh_ref.dtype)
        vh = v[:, sl]
        vf_ref[:, sl] = vh
        if key_chunk is None:
            vx_ref[hh] = vh.astype(vx_ref.dtype)
        else:
            vt = vh.T.astype(vx_ref.dtype)
            for cc in range(tm // key_chunk):
                vx_ref[hh, cc] = vt[:, cc * key_chunk:(cc + 1) * key_chunk]


def _qkv(h, w_qkv, g_q, g_k, key_chunk):
    m, d = h.shape
    nh = d // HEAD_DIM
    tn = _tile(d, 512)
    hpt = tn // HEAD_DIM
    nj = d // tn
    tm = _tile(m, 512)
    hm = jax.ShapeDtypeStruct((nh, m, HEAD_DIM), _BF16)
    hm_spec = pl.BlockSpec((hpt, tm, HEAD_DIM), lambda j, i: (j, i, 0))
    if key_chunk is None:
        vx, vx_spec = hm, hm_spec
    else:
        vx = jax.ShapeDtypeStruct((nh, m // key_chunk, HEAD_DIM, key_chunk), _BF16)
        vx_spec = pl.BlockSpec((hpt, tm // key_chunk, HEAD_DIM, key_chunk), lambda j, i: (j, i, 0, 0))
    blocks = (_nbytes((tm, d), _BF16) + 3 * _nbytes((d, tn), _BF16)
              + 3 * _nbytes((tm, tn), _BF16) + 2 * _nbytes((tm, tn), _F32))
    return pl.pallas_call(
        functools.partial(_qkv_kernel, key_chunk=key_chunk, q_scale=SB_SCALE * _LOG2E),
        grid=(nj, m // tm),
        in_specs=[
            pl.BlockSpec((tm, d), lambda j, i: (i, 0)),
            pl.BlockSpec((d, tn), lambda j, i: (0, j)),
            pl.BlockSpec((d, tn), lambda j, i: (0, j + nj)),
            pl.BlockSpec((d, tn), lambda j, i: (0, j + 2 * nj)),
            pl.BlockSpec((1, HEAD_DIM), lambda j, i: (0, 0)),
            pl.BlockSpec((1, HEAD_DIM), lambda j, i: (0, 0)),
        ],
        out_specs=[hm_spec, hm_spec, vx_spec,
                   pl.BlockSpec((tm, tn), lambda j, i: (i, j)), pl.BlockSpec((tm, tn), lambda j, i: (i, j))],
        out_shape=[hm, hm, vx, jax.ShapeDtypeStruct((m, d), _F32), jax.ShapeDtypeStruct((m, d), _F32)],
        compiler_params=_params(2, blocks, 8 * _nbytes((tm, tn), _F32)),
        name="qkv",
    )(h, w_qkv, w_qkv, w_qkv, g_q.reshape(1, HEAD_DIM), g_k.reshape(1, HEAD_DIM))


def _sb_logits(kspan, q, tri, r_ref, masks):
    tk = tri.shape[0]
    z = lax.dot_general(kspan, q, (((1,), (1,)), ((), ())), preferred_element_type=_F32)
    neg_abs = lax.bitcast_convert_type(lax.bitcast_convert_type(z, jnp.int32) | _SIGN_BIT, _F32)
    sp = jnp.maximum(z, 0.0) + jnp.log(1.0 + jnp.exp2(neg_abs)) * _LOG2E
    r = r_ref[0:1, :]
    w_blocks = [None] * len(masks)
    for b in reversed(range(len(masks))):
        zb, spb = z[b * tk:(b + 1) * tk], sp[b * tk:(b + 1) * tk]
        if masks[b]:
            causal = lax.broadcasted_iota(jnp.int32, zb.shape, 0) < lax.broadcasted_iota(jnp.int32, zb.shape, 1)
            spb = jnp.where(causal, spb, 0.0)
        cs = _dot(tri, spb.astype(_BF16))
        wb = zb - cs + r
        w_blocks[b] = jnp.where(causal, wb, -jnp.inf) if masks[b] else wb
        r = r - cs[0:1, :]
    r_ref[0:1, :] = r
    return w_blocks[0] if len(masks) == 1 else jnp.concatenate(w_blocks, axis=0)


def _sb_accumulate(w, vt, acc_ref):
    acc_ref[...] += _dot(vt, jnp.exp2(w).astype(_BF16))


def _attn_prompt_kernel(q_ref, k_ref, vt_ref, tri_ref, o_ref, r_ref, acc_ref, w_ref, *, tk, heads, nsub):
    qi = pl.program_id(1)
    tri = tri_ref[...]
    chunk = nsub * tk
    chains = [(h, c) for h in range(heads) for c in range(nsub)]
    r_ref[...] = jnp.zeros_like(r_ref)
    acc_ref[...] = jnp.zeros_like(acc_ref)
    qs = {(h, c): q_ref[h, c * tk:(c + 1) * tk, :] for h, c in chains}

    dstart = pl.multiple_of(qi * chunk, chunk)
    for n, (h, c) in enumerate(chains):
        w = _sb_logits(k_ref[h, pl.ds(dstart, (c + 1) * tk), :], qs[h, c], tri, r_ref.at[n],
                       (False,) * c + (True,))
        if c + 1 < nsub:
            w = jnp.concatenate([w, jnp.full(((nsub - c - 1) * tk, tk), -jnp.inf, _F32)], axis=0)
        w_ref[n] = w

    def logits(kc):
        start = pl.multiple_of(kc * chunk, chunk)
        for n, (h, c) in enumerate(chains):
            w_ref[n] = _sb_logits(k_ref[h, pl.ds(start, chunk), :], qs[h, c], tri, r_ref.at[n], (False,) * nsub)

    def accumulate(kc):
        for n, (h, c) in enumerate(chains):
            _sb_accumulate(w_ref[n], vt_ref[h, kc], acc_ref.at[n])

    def body(t, carry):
        accumulate(qi - t)
        logits(qi - 1 - t)
        return carry

    lax.fori_loop(0, qi, body, 0)
    accumulate(0)
    for n, (h, c) in enumerate(chains):
        o_ref[c * tk:(c + 1) * tk, h * HEAD_DIM:(h + 1) * HEAD_DIM] = acc_ref[n].T.astype(o_ref.dtype)


def _tri(tk):
    idx = jnp.arange(tk)
    return (idx[None, :] >= idx[:, None]).astype(_BF16)


def _attn_prompt(q_hm, k_hm, vt, tk, heads):
    nh, m, hd = q_hm.shape
    chunk = vt.shape[3]
    nsub = chunk // tk
    resident = dict(pipeline_mode=pl.Buffered(1))
    blocks = (2 * _nbytes((heads, chunk, hd), _BF16) + _nbytes((heads, m, hd), _BF16)
              + _nbytes((tk, tk), _BF16))
    return pl.pallas_call(
        functools.partial(_attn_prompt_kernel, tk=tk, heads=heads, nsub=nsub),
        grid=(nh // heads, m // chunk),
        in_specs=[
            pl.BlockSpec((heads, chunk, hd), lambda g, i: (g, i, 0)),
            pl.BlockSpec((heads, m, hd), lambda g, i: (g, 0, 0), **resident),
            pl.BlockSpec((heads, m // chunk, hd, chunk), lambda g, i: (g, 0, 0, 0), **resident),
            pl.BlockSpec((tk, tk), lambda g, i: (0, 0)),
        ],
        out_specs=pl.BlockSpec((chunk, heads * hd), lambda g, i: (i, g)),
        out_shape=jax.ShapeDtypeStruct((m, nh * hd), _BF16),
        scratch_shapes=[pltpu.VMEM((heads * nsub, _SUBLANES, tk), _F32), pltpu.VMEM((heads * nsub, hd, tk), _F32),
                        pltpu.VMEM((heads * nsub, chunk, tk), _F32)],
        compiler_params=_params(2, blocks, 17 * heads * nsub * _nbytes((chunk, tk), _F32)),
        name="sb_attn_prompt",
    )(q_hm, k_hm, vt, _tri(tk))


def _attn_sample_kernel(q_ref, kn_ref, vn_ref, ck_ref, cv_ref, tri_ref, o_ref, r_ref, acc_ref, w_ref,
                        *, tk, tq, heads, nsub):
    t, hd = q_ref.shape[1], q_ref.shape[2]
    past = ck_ref.shape[0]
    chunk = nsub * tk
    n_chunks = past // chunk
    tri = tri_ref[...]
    r_ref[...] = jnp.zeros_like(r_ref)
    acc_ref[...] = jnp.zeros_like(acc_ref)
    qs = [jnp.concatenate([q_ref[h], jnp.zeros((tq - t, hd), _BF16)], axis=0) for h in range(heads)]
    for h in range(heads):
        k_new = jnp.concatenate([kn_ref[h], jnp.zeros((tk - t, hd), _BF16)], axis=0)
        v_new = jnp.concatenate([vn_ref[h].astype(_F32), jnp.zeros((tk - t, hd), _F32)], axis=0)
        w = _sb_logits(k_new, qs[h], tri, r_ref.at[h], (True,))
        _sb_accumulate(w, v_new.T.astype(_BF16), acc_ref.at[h])

    def logits(kc):
        start = pl.multiple_of(kc * chunk, chunk)
        for h in range(heads):
            kspan = ck_ref[pl.ds(start, chunk), h * hd:(h + 1) * hd]
            w_ref[h] = _sb_logits(kspan, qs[h], tri, r_ref.at[h], (False,) * nsub)

    def accumulate(kc):
        start = pl.multiple_of(kc * chunk, chunk)
        for h in range(heads):
            vt = cv_ref[pl.ds(start, chunk), h * hd:(h + 1) * hd].T
            _sb_accumulate(w_ref[h], vt, acc_ref.at[h])

    logits(n_chunks - 1)

    def body(i, carry):
        accumulate(n_chunks - 1 - i)
        logits(n_chunks - 2 - i)
        return carry

    lax.fori_loop(0, n_chunks - 1, body, 0)
    accumulate(0)
    for h in range(heads):
        o_ref[:, h * hd:(h + 1) * hd] = acc_ref[h].T[0:t, :].astype(o_ref.dtype)


def _attn_sample(q_hm, k_hm, v_hm, cache_k, cache_v, layer, tk, chunk, heads):
    nh, m, hd = q_hm.shape
    n_streams, past = cache_k.shape[1], cache_k.shape[2]
    t = m // n_streams
    assert past % chunk == 0 and t <= HEAD_DIM
    tq = HEAD_DIM
    new_spec = pl.BlockSpec((heads, t, hd), lambda s, g: (g, s, 0))
    cache_spec = pl.BlockSpec((None, None, past, heads * hd), lambda s, g: (layer, s, 0, g))
    blocks = (3 * _nbytes((heads, t, hd), _BF16) + 2 * _nbytes((past, heads * hd), _BF16)
              + _nbytes((tk, tk), _BF16))
    return pl.pallas_call(
        functools.partial(_attn_sample_kernel, tk=tk, tq=tq, heads=heads, nsub=chunk // tk),
        grid=(n_streams, nh // heads),
        in_specs=[new_spec, new_spec, new_spec, cache_spec, cache_spec, pl.BlockSpec((tk, tk), lambda s, g: (0, 0))],
        out_specs=pl.BlockSpec((t, heads * hd), lambda s, g: (s, g)),
        out_shape=jax.ShapeDtypeStruct((m, nh * hd), _BF16),
        scratch_shapes=[pltpu.VMEM((heads, _SUBLANES, tq), _F32), pltpu.VMEM((heads, hd, tq), _F32),
                        pltpu.VMEM((heads, chunk, tq), _F32)],
        compiler_params=_params(2, blocks, 17 * heads * _nbytes((chunk, tq), _F32)),
        name="sb_attn_sample",
    )(q_hm, k_hm, v_hm, cache_k, cache_v, _tri(tk))


def _ffn(x, g, wg, wu, wd):
    mid = _ffn_up(_rmsnorm(x, g), wg, wu)
    return _matmul_res(mid, wd, x, 512, 256, w_resident=False)


def kernel(x_prompt, x_sample, state_conv, cache_k, cache_v, g_mix, g_ffn, w_conv_in, w_conv, w_conv_out,
           w_qkv, g_q, g_k, w_o, w_gate, w_up, w_down):
    batch, seq, d = x_prompt.shape
    dec_batch, dec_seq, _ = x_sample.shape
    assert batch == 1, "the prompt path handles one fresh stream"
    nh = d // HEAD_DIM
    depth = g_mix.shape[0]
    attn_tk, attn_chunk = 256, 512
    past = cache_k.shape[2]
    cache_k = cache_k.astype(_BF16).reshape(cache_k.shape[0], dec_batch, past, d)
    cache_v = cache_v.astype(_BF16).reshape(cache_v.shape[0], dec_batch, past, d)
    xp = x_prompt.reshape(seq, d)
    xs = x_sample.reshape(dec_batch * dec_seq, d)
    conv_p, conv_s, k_p, v_p, k_s, v_s = [], [], [], [], [], []
    for i in range(depth):
        j = i // N_MIXERS
        hp = _rmsnorm(xp, g_mix[i])
        hs = _rmsnorm(xs, g_mix[i])
        if i % N_MIXERS == 0:
            w_in = w_conv_in[j].astype(_BF16)
            w_out = w_conv_out[j].astype(_BF16)
            gp, tail_p = _conv_in(hp, w_in, w_conv[j], jnp.zeros((CONV_W - 1, d), _F32), None)
            st = state_conv[j]
            inject = jnp.zeros((2, dec_batch, dec_seq, d), _F32)
            inject = inject.at[0, :, 0].set(st[:, 1]).at[1, :, 0].set(st[:, 0]).at[1, :, 1].set(st[:, 1])
            gs, u_s = _conv_in(hs, w_in, w_conv[j], inject.reshape(2, dec_batch * dec_seq, d), dec_seq)
            conv_p.append(tail_p[_SUBLANES - (CONV_W - 1):].reshape(1, CONV_W - 1, d))
            conv_s.append(u_s.reshape(dec_batch, dec_seq, d)[:, dec_seq - (CONV_W - 1):])
            xp = _matmul_res(gp, w_out, xp, 1024, 512, w_resident=True)
            xs = _matmul_res(gs, w_out, xs, 1024, 512, w_resident=True)
        else:
            wqkv = w_qkv[j].astype(_BF16)
            wo = w_o[j].astype(_BF16)
            qp, kp, vtp, kfp, vfp = _qkv(hp, wqkv, g_q[j], g_k[j], attn_chunk)
            qs, ks, vs, kfs, vfs = _qkv(hs, wqkv, g_q[j], g_k[j], None)
            op = _attn_prompt(qp, kp, vtp, attn_tk, heads=4)
            os_ = _attn_sample(qs, ks, vs, cache_k, cache_v, j, attn_tk, attn_chunk, heads=8)
            k_p.append(kfp.reshape(batch, seq, nh, HEAD_DIM))
            v_p.append(vfp.reshape(batch, seq, nh, HEAD_DIM))
            k_s.append(kfs.reshape(dec_batch, dec_seq, nh, HEAD_DIM))
            v_s.append(vfs.reshape(dec_batch, dec_seq, nh, HEAD_DIM))
            xp = _matmul_res(op, wo, xp, 1024, 512, w_resident=True)
            xs = _matmul_res(os_, wo, xs, 1024, 512, w_resident=True)
        wg, wu, wd = w_gate[i].astype(_BF16), w_up[i].astype(_BF16), w_down[i].astype(_BF16)
        xp = _ffn(xp, g_ffn[i], wg, wu, wd)
        xs = _ffn(xs, g_ffn[i], wg, wu, wd)
    return (xp.reshape(batch, seq, d), xs.reshape(dec_batch, dec_seq, d),
            jnp.stack(conv_p), jnp.stack(conv_s), jnp.stack(k_p), jnp.stack(v_p), jnp.stack(k_s), jnp.stack(v_s))
```

```python
import functools

import jax
import jax.numpy as jnp
from jax import lax
from jax.experimental import pallas as pl
from jax.experimental.pallas import tpu as pltpu

EPS = 1e-6
HEAD_DIM = 128
CONV_W = 3
N_MIXERS = 2
SB_SCALE = HEAD_DIM ** -0.5
_LOG2E = 1.4426950408889634
_SIGN_BIT = -2 ** 31
_DEAD_LOG2 = -160.0

_BF16 = jnp.bfloat16
_F32 = jnp.float32
_SUBLANES = 8
_V7X_VMEM_BYTES = 64 * 1024 * 1024
_VMEM_CAP_BYTES = _V7X_VMEM_BYTES - 6 * 1024 * 1024


def _nbytes(shape, dtype):
    n = jnp.dtype(dtype).itemsize
    for s in shape:
        n *= s
    return n


def _params(n_grid, block_bytes, extra_bytes=0):
    est = 2 * block_bytes + extra_bytes + 4 * 1024 * 1024
    return pltpu.CompilerParams(
        dimension_semantics=("arbitrary",) * n_grid,
        vmem_limit_bytes=int(min(max(est, 16 * 1024 * 1024), _VMEM_CAP_BYTES)),
    )


def _tile(dim, want):
    t = min(dim, want)
    while dim % t:
        t -= 1
    return t


def _dot(a, b):
    return jnp.dot(a, b, preferred_element_type=_F32)


def _rmsnorm_kernel(x_ref, g_ref, o_ref):
    x = x_ref[...]
    ms = jnp.mean(x * x, axis=-1, keepdims=True)
    o_ref[...] = (x * lax.rsqrt(ms + EPS) * g_ref[...]).astype(o_ref.dtype)


def _rmsnorm(x, g):
    m, d = x.shape
    tm = _tile(m, 256)
    blocks = _nbytes((tm, d), _F32) + _nbytes((tm, d), _BF16)
    return pl.pallas_call(
        _rmsnorm_kernel,
        grid=(m // tm,),
        in_specs=[pl.BlockSpec((tm, d), lambda i: (i, 0)), pl.BlockSpec((1, d), lambda i: (0, 0))],
        out_specs=pl.BlockSpec((tm, d), lambda i: (i, 0)),
        out_shape=jax.ShapeDtypeStruct((m, d), _BF16),
        compiler_params=_params(1, blocks, _nbytes((tm, d), _F32)),
        name="rmsnorm",
    )(x, g.reshape(1, d))


def _conv_in_kernel(h_ref, wb_ref, wc_ref, wx_ref, wconv_ref, st_ref, g_ref, tail_ref, u_scr,
                    *, seg_len, tail_rows):
    i = pl.program_id(1)
    hb = h_ref[...]
    b = _dot(hb, wb_ref[...])
    c = _dot(hb, wc_ref[...])
    xin = _dot(hb, wx_ref[...])
    u = c * xin
    tm = u.shape[0]
    u_scr[_SUBLANES:_SUBLANES + tm, :] = u
    if seg_len is None:
        @pl.when(i == 0)
        def _():
            u_scr[0:_SUBLANES - 2, :] = jnp.zeros((_SUBLANES - 2, u.shape[1]), _F32)
            u_scr[_SUBLANES - 2:_SUBLANES, :] = st_ref[...]
        u1 = u_scr[_SUBLANES - 1:_SUBLANES - 1 + tm, :]
        u2 = u_scr[_SUBLANES - 2:_SUBLANES - 2 + tm, :]
    else:
        u_scr[0:_SUBLANES, :] = jnp.zeros((_SUBLANES, u.shape[1]), _F32)
        r = lax.broadcasted_iota(jnp.int32, u.shape, 0) % seg_len
        u1 = jnp.where(r == 0, st_ref[0], u_scr[_SUBLANES - 1:_SUBLANES - 1 + tm, :])
        u2 = jnp.where(r < 2, st_ref[1], u_scr[_SUBLANES - 2:_SUBLANES - 2 + tm, :])
    conv = u2 * wconv_ref[0:1, :] + u1 * wconv_ref[1:2, :] + u * wconv_ref[2:3, :]
    g_ref[...] = (b * conv).astype(g_ref.dtype)
    tail_ref[...] = u[tm - tail_rows:, :]
    if seg_len is None:
        u_scr[0:_SUBLANES, :] = u[tm - _SUBLANES:, :]


def _conv_in(h, w_in, w_conv, state, seg_len):
    m, d = h.shape
    tn = _tile(d, 512)
    nj = d // tn
    if seg_len is None:
        tm = _tile(m, 512)
        tail_rows, tail_shape = _SUBLANES, (_SUBLANES, d)
        st_spec = pl.BlockSpec((2, tn), lambda j, i: (0, j))
        tail_spec = pl.BlockSpec((_SUBLANES, tn), lambda j, i: (0, j))
        st_bytes = _nbytes((2, tn), _F32)
    else:
        tm = m
        tail_rows, tail_shape = m, (m, d)
        st_spec = pl.BlockSpec((2, tm, tn), lambda j, i: (0, 0, j))
        tail_spec = pl.BlockSpec((tm, tn), lambda j, i: (0, j))
        st_bytes = _nbytes((2, tm, tn), _F32)
    blocks = (_nbytes((tm, d), _BF16) + 3 * _nbytes((d, tn), _BF16) + st_bytes
              + _nbytes((tm, tn), _BF16) + _nbytes((tail_rows, tn), _F32))
    scratch = _nbytes((tm + _SUBLANES, tn), _F32)
    return pl.pallas_call(
        functools.partial(_conv_in_kernel, seg_len=seg_len, tail_rows=tail_rows),
        grid=(nj, m // tm),
        in_specs=[
            pl.BlockSpec((tm, d), lambda j, i: (i, 0)),
            pl.BlockSpec((d, tn), lambda j, i: (0, j)),
            pl.BlockSpec((d, tn), lambda j, i: (0, j + nj)),
            pl.BlockSpec((d, tn), lambda j, i: (0, j + 2 * nj)),
            pl.BlockSpec((CONV_W, tn), lambda j, i: (0, j)),
            st_spec,
        ],
        out_specs=[pl.BlockSpec((tm, tn), lambda j, i: (i, j)), tail_spec],
        out_shape=[jax.ShapeDtypeStruct((m, d), _BF16), jax.ShapeDtypeStruct(tail_shape, _F32)],
        scratch_shapes=[pltpu.VMEM((tm + _SUBLANES, tn), _F32)],
        compiler_params=_params(2, blocks, scratch + 6 * _nbytes((tm, tn), _F32)),
        name="conv_in",
    )(h, w_in, w_in, w_in, w_conv, state)


def _matmul_res_kernel(a_ref, w_ref, r_ref, o_ref):
    o_ref[...] = _dot(a_ref[...], w_ref[...]) + r_ref[...]


def _matmul_res(a, w, res, tm_want, tn_want, w_resident):
    m, k = a.shape
    n = w.shape[1]
    tm, tn = _tile(m, tm_want), _tile(n, tn_want)
    if w_resident:
        grid = (n // tn, m // tm)
        a_map, w_map, o_map = (lambda j, i: (i, 0)), (lambda j, i: (0, j)), (lambda j, i: (i, j))
    else:
        grid = (m // tm, n // tn)
        a_map, w_map, o_map = (lambda i, j: (i, 0)), (lambda i, j: (0, j)), (lambda i, j: (i, j))
    blocks = _nbytes((tm, k), _BF16) + _nbytes((k, tn), _BF16) + 2 * _nbytes((tm, tn), _F32)
    return pl.pallas_call(
        _matmul_res_kernel,
        grid=grid,
        in_specs=[pl.BlockSpec((tm, k), a_map), pl.BlockSpec((k, tn), w_map), pl.BlockSpec((tm, tn), o_map)],
        out_specs=pl.BlockSpec((tm, tn), o_map),
        out_shape=jax.ShapeDtypeStruct((m, n), _F32),
        compiler_params=_params(2, blocks, 2 * _nbytes((tm, tn), _F32)),
        name="matmul_res",
    )(a, w, res)


def _ffn_up_kernel(h_ref, wg_ref, wu_ref, o_ref):
    hb = h_ref[...]
    gate = _dot(hb, wg_ref[...])
    up = _dot(hb, wu_ref[...])
    o_ref[...] = (gate * jax.nn.sigmoid(gate) * up).astype(o_ref.dtype)


def _ffn_up(h, wg, wu):
    m, d = h.shape
    f = wg.shape[1]
    tm, tn = _tile(m, 2048), _tile(f, 256)
    blocks = _nbytes((tm, d), _BF16) + 2 * _nbytes((d, tn), _BF16) + _nbytes((tm, tn), _BF16)
    return pl.pallas_call(
        _ffn_up_kernel,
        grid=(m // tm, f // tn),
        in_specs=[
            pl.BlockSpec((tm, d), lambda i, j: (i, 0)),
            pl.BlockSpec((d, tn), lambda i, j: (0, j)),
            pl.BlockSpec((d, tn), lambda i, j: (0, j)),
        ],
        out_specs=pl.BlockSpec((tm, tn), lambda i, j: (i, j)),
        out_shape=jax.ShapeDtypeStruct((m, f), _BF16),
        compiler_params=_params(2, blocks, 4 * _nbytes((tm, tn), _F32)),
        name="ffn_up",
    )(h, wg, wu)


def _qkv_kernel(h_ref, wq_ref, wk_ref, wv_ref, gq_ref, gk_ref, qh_ref, kh_ref, vx_ref, kf_ref, vf_ref,
                *, key_chunk, q_scale):
    hb = h_ref[...]
    q = _dot(hb, wq_ref[...])
    k = _dot(hb, wk_ref[...])
    v = _dot(hb, wv_ref[...])
    tm = q.shape[0]

    def head_norm(x, g):
        y = x * lax.rsqrt(jnp.mean(x * x, axis=-1, keepdims=True) + EPS)
        return y * g

    for hh in range(q.shape[1] // HEAD_DIM):
        sl = slice(hh * HEAD_DIM, (hh + 1) * HEAD_DIM)
        qh_ref[hh] = (head_norm(q[:, sl], gq_ref[...]) * q_scale).astype(qh_ref.dtype)
        kn = head_norm(k[:, sl], gk_ref[...])
        kf_ref[:, sl] = kn
        kh_ref[hh] = kn.astype(kh_ref.dtype)
        vh = v[:, sl]
        vf_ref[:, sl] = vh
        if key_chunk is None:
            vx_ref[hh] = vh.astype(vx_ref.dtype)
        else:
            vt = vh.T.astype(vx_ref.dtype)
            for cc in range(tm // key_chunk):
                vx_ref[hh, cc] = vt[:, cc * key_chunk:(cc + 1) * key_chunk]


def _qkv(h, w_qkv, g_q, g_k, key_chunk):
    m, d = h.shape
    nh = d // HEAD_DIM
    tn = _tile(d, 512)
    hpt = tn // HEAD_DIM
    nj = d // tn
    tm = _tile(m, 512)
    hm = jax.ShapeDtypeStruct((nh, m, HEAD_DIM), _BF16)
    hm_spec = pl.BlockSpec((hpt, tm, HEAD_DIM), lambda j, i: (j, i, 0))
    if key_chunk is None:
        vx, vx_spec = hm, hm_spec
    else:
        vx = jax.ShapeDtypeStruct((nh, m // key_chunk, HEAD_DIM, key_chunk), _BF16)
        vx_spec = pl.BlockSpec((hpt, tm // key_chunk, HEAD_DIM, key_chunk), lambda j, i: (j, i, 0, 0))
    blocks = (_nbytes((tm, d), _BF16) + 3 * _nbytes((d, tn), _BF16)
              + 3 * _nbytes((tm, tn), _BF16) + 2 * _nbytes((tm, tn), _F32))
    return pl.pallas_call(
        functools.partial(_qkv_kernel, key_chunk=key_chunk, q_scale=SB_SCALE * _LOG2E),
        grid=(nj, m // tm),
        in_specs=[
            pl.BlockSpec((tm, d), lambda j, i: (i, 0)),
            pl.BlockSpec((d, tn), lambda j, i: (0, j)),
            pl.BlockSpec((d, tn), lambda j, i: (0, j + nj)),
            pl.BlockSpec((d, tn), lambda j, i: (0, j + 2 * nj)),
            pl.BlockSpec((1, HEAD_DIM), lambda j, i: (0, 0)),
            pl.BlockSpec((1, HEAD_DIM), lambda j, i: (0, 0)),
        ],
        out_specs=[hm_spec, hm_spec, vx_spec,
                   pl.BlockSpec((tm, tn), lambda j, i: (i, j)), pl.BlockSpec((tm, tn), lambda j, i: (i, j))],
        out_shape=[hm, hm, vx, jax.ShapeDtypeStruct((m, d), _F32), jax.ShapeDtypeStruct((m, d), _F32)],
        compiler_params=_params(2, blocks, 8 * _nbytes((tm, tn), _F32)),
        name="qkv",
    )(h, w_qkv, w_qkv, w_qkv, g_q.reshape(1, HEAD_DIM), g_k.reshape(1, HEAD_DIM))


def _sb_logits(kspan, q, tri, r_ref, masks):
    tk = tri.shape[0]
    z = lax.dot_general(kspan, q, (((1,), (1,)), ((), ())), preferred_element_type=_F32)
    neg_abs = lax.bitcast_convert_type(lax.bitcast_convert_type(z, jnp.int32) | _SIGN_BIT, _F32)
    sp = jnp.maximum(z, 0.0) + jnp.log(1.0 + jnp.exp2(neg_abs)) * _LOG2E
    r = r_ref[0:1, :]
    w_blocks = [None] * len(masks)
    for b in reversed(range(len(masks))):
        zb, spb = z[b * tk:(b + 1) * tk], sp[b * tk:(b + 1) * tk]
        if masks[b]:
            causal = lax.broadcasted_iota(jnp.int32, zb.shape, 0) < lax.broadcasted_iota(jnp.int32, zb.shape, 1)
            spb = jnp.where(causal, spb, 0.0)
        cs = _dot(tri, spb.astype(_BF16))
        wb = jnp.minimum(zb - cs, 0.0) + r
        w_blocks[b] = jnp.where(causal, wb, -jnp.inf) if masks[b] else wb
        r = r - cs[0:1, :]
    r_ref[0:1, :] = r
    return w_blocks[0] if len(masks) == 1 else jnp.concatenate(w_blocks, axis=0)


def _sb_alive(r_ref):
    return jnp.max(r_ref[:, 0, :]) >= _DEAD_LOG2


def _sb_accumulate(w, vt, acc_ref):
    acc_ref[...] += _dot(vt, jnp.exp2(w).astype(_BF16))


def _attn_prompt_kernel(q_ref, k_ref, vt_ref, tri_ref, o_ref, r_ref, acc_ref, w_ref, *, tk, heads, nsub):
    qi = pl.program_id(1)
    tri = tri_ref[...]
    chunk = nsub * tk
    chains = [(h, c) for h in range(heads) for c in range(nsub)]
    r_ref[...] = jnp.zeros_like(r_ref)
    acc_ref[...] = jnp.zeros_like(acc_ref)
    qs = {(h, c): q_ref[h, c * tk:(c + 1) * tk, :] for h, c in chains}

    dstart = pl.multiple_of(qi * chunk, chunk)
    for n, (h, c) in enumerate(chains):
        w = _sb_logits(k_ref[h, pl.ds(dstart, (c + 1) * tk), :], qs[h, c], tri, r_ref.at[n],
                       (False,) * c + (True,))
        if c + 1 < nsub:
            w = jnp.concatenate([w, jnp.full(((nsub - c - 1) * tk, tk), -jnp.inf, _F32)], axis=0)
        w_ref[n] = w

    def logits(kc):
        start = pl.multiple_of(kc * chunk, chunk)
        for n, (h, c) in enumerate(chains):
            w_ref[n] = _sb_logits(k_ref[h, pl.ds(start, chunk), :], qs[h, c], tri, r_ref.at[n], (False,) * nsub)

    def accumulate(kc):
        for n, (h, c) in enumerate(chains):
            _sb_accumulate(w_ref[n], vt_ref[h, kc], acc_ref.at[n])

    def body(carry):
        t, _ = carry
        accumulate(qi - t)
        logits(qi - 1 - t)
        return t + 1, _sb_alive(r_ref)

    t, _ = lax.while_loop(lambda carry: (carry[0] < qi) & carry[1], body, (jnp.int32(0), _sb_alive(r_ref)))
    accumulate(qi - t)
    for n, (h, c) in enumerate(chains):
        o_ref[c * tk:(c + 1) * tk, h * HEAD_DIM:(h + 1) * HEAD_DIM] = acc_ref[n].T.astype(o_ref.dtype)


def _tri(tk):
    idx = jnp.arange(tk)
    return (idx[None, :] >= idx[:, None]).astype(_BF16)


def _attn_prompt(q_hm, k_hm, vt, tk, heads):
    nh, m, hd = q_hm.shape
    chunk = vt.shape[3]
    nsub = chunk // tk
    resident = dict(pipeline_mode=pl.Buffered(1))
    blocks = (2 * _nbytes((heads, chunk, hd), _BF16) + _nbytes((heads, m, hd), _BF16)
              + _nbytes((tk, tk), _BF16))
    return pl.pallas_call(
        functools.partial(_attn_prompt_kernel, tk=tk, heads=heads, nsub=nsub),
        grid=(nh // heads, m // chunk),
        in_specs=[
            pl.BlockSpec((heads, chunk, hd), lambda g, i: (g, i, 0)),
            pl.BlockSpec((heads, m, hd), lambda g, i: (g, 0, 0), **resident),
            pl.BlockSpec((heads, m // chunk, hd, chunk), lambda g, i: (g, 0, 0, 0), **resident),
            pl.BlockSpec((tk, tk), lambda g, i: (0, 0)),
        ],
        out_specs=pl.BlockSpec((chunk, heads * hd), lambda g, i: (i, g)),
        out_shape=jax.ShapeDtypeStruct((m, nh * hd), _BF16),
        scratch_shapes=[pltpu.VMEM((heads * nsub, _SUBLANES, tk), _F32), pltpu.VMEM((heads * nsub, hd, tk), _F32),
                        pltpu.VMEM((heads * nsub, chunk, tk), _F32)],
        compiler_params=_params(2, blocks, 17 * heads * nsub * _nbytes((chunk, tk), _F32)),
        name="sb_attn_prompt",
    )(q_hm, k_hm, vt, _tri(tk))


def _attn_sample_kernel(q_ref, kn_ref, vn_ref, ck_ref, cv_ref, tri_ref, o_ref, r_ref, acc_ref, w_ref,
                        *, tk, tq, nsub):
    step = pl.program_id(2)
    heads, t, hd = q_ref.shape
    chunk = nsub * tk
    n_chunks = ck_ref.shape[0] // chunk
    tri = tri_ref[...]
    qs = [jnp.concatenate([q_ref[h], jnp.zeros((tq - t, hd), _BF16)], axis=0) for h in range(heads)]

    @pl.when(step == 0)
    def _():
        r_ref[...] = jnp.zeros_like(r_ref)
        acc_ref[...] = jnp.zeros_like(acc_ref)
        for h in range(heads):
            k_new = jnp.concatenate([kn_ref[h], jnp.zeros((tk - t, hd), _BF16)], axis=0)
            v_new = jnp.concatenate([vn_ref[h].astype(_F32), jnp.zeros((tk - t, hd), _F32)], axis=0)
            w = _sb_logits(k_new, qs[h], tri, r_ref.at[h], (True,))
            _sb_accumulate(w, v_new.T.astype(_BF16), acc_ref.at[h])

    def logits(kc):
        start = pl.multiple_of(kc * chunk, chunk)
        for h in range(heads):
            kspan = ck_ref[pl.ds(start, chunk), h, :].astype(_BF16)
            w_ref[h] = _sb_logits(kspan, qs[h], tri, r_ref.at[h], (False,) * nsub)

    def accumulate(kc):
        start = pl.multiple_of(kc * chunk, chunk)
        for h in range(heads):
            vt = cv_ref[pl.ds(start, chunk), h, :].T.astype(_BF16)
            _sb_accumulate(w_ref[h], vt, acc_ref.at[h])

    @pl.when(_sb_alive(r_ref))
    def _():
        logits(n_chunks - 1)

        def body(carry):
            i, _ = carry
            accumulate(n_chunks - 1 - i)
            logits(n_chunks - 2 - i)
            return i + 1, _sb_alive(r_ref)

        i, _ = lax.while_loop(lambda carry: (carry[0] < n_chunks - 1) & carry[1], body,
                              (jnp.int32(0), _sb_alive(r_ref)))
        accumulate(n_chunks - 1 - i)

    @pl.when(step == pl.num_programs(2) - 1)
    def _():
        for h in range(heads):
            o_ref[:, h * hd:(h + 1) * hd] = acc_ref[h].T[0:t, :].astype(o_ref.dtype)


def _attn_sample(q_hm, k_hm, v_hm, cache_k, cache_v, layer, tk, chunk, span):
    nh, m, hd = q_hm.shape
    n_streams, past = cache_k.shape[1], cache_k.shape[2]
    t = m // n_streams
    heads = _SUBLANES
    assert past % span == 0 and span % chunk == 0 and nh % heads == 0 and t <= HEAD_DIM
    tq = HEAD_DIM
    n_spans = past // span
    grouped = (cache_k.shape[0], n_streams, past, nh // heads, heads, hd)
    new_spec = pl.BlockSpec((heads, t, hd), lambda s, g, c: (g, s, 0))
    cache_spec = pl.BlockSpec((None, None, span, None, heads, hd),
                              lambda s, g, c: (layer, s, n_spans - 1 - c, g, 0, 0))
    blocks = (3 * _nbytes((heads, t, hd), _BF16) + 2 * _nbytes((span, heads, hd), _F32)
              + _nbytes((tk, tk), _BF16))
    return pl.pallas_call(
        functools.partial(_attn_sample_kernel, tk=tk, tq=tq, nsub=chunk // tk),
        grid=(n_streams, nh // heads, n_spans),
        in_specs=[new_spec, new_spec, new_spec, cache_spec, cache_spec,
                  pl.BlockSpec((tk, tk), lambda s, g, c: (0, 0))],
        out_specs=pl.BlockSpec((t, heads * hd), lambda s, g, c: (s, g)),
        out_shape=jax.ShapeDtypeStruct((m, nh * hd), _BF16),
        scratch_shapes=[pltpu.VMEM((heads, _SUBLANES, tq), _F32), pltpu.VMEM((heads, hd, tq), _F32),
                        pltpu.VMEM((heads, chunk, tq), _F32)],
        compiler_params=_params(3, blocks, 17 * heads * _nbytes((chunk, tq), _F32)),
        name="sb_attn_sample",
    )(q_hm, k_hm, v_hm, cache_k.reshape(grouped), cache_v.reshape(grouped), _tri(tk))


def _ffn(x, g, wg, wu, wd):
    mid = _ffn_up(_rmsnorm(x, g), wg, wu)
    return _matmul_res(mid, wd, x, 512, 256, w_resident=False)


def kernel(x_prompt, x_sample, state_conv, cache_k, cache_v, g_mix, g_ffn, w_conv_in, w_conv, w_conv_out,
           w_qkv, g_q, g_k, w_o, w_gate, w_up, w_down):
    batch, seq, d = x_prompt.shape
    dec_batch, dec_seq, _ = x_sample.shape
    assert batch == 1, "the prompt path handles one fresh stream"
    nh = d // HEAD_DIM
    depth = g_mix.shape[0]
    attn_tk, attn_chunk = 256, 512
    xp = x_prompt.reshape(seq, d)
    xs = x_sample.reshape(dec_batch * dec_seq, d)
    conv_p, conv_s, k_p, v_p, k_s, v_s = [], [], [], [], [], []
    for i in range(depth):
        j = i // N_MIXERS
        hp = _rmsnorm(xp, g_mix[i])
        hs = _rmsnorm(xs, g_mix[i])
        if i % N_MIXERS == 0:
            w_in = w_conv_in[j].astype(_BF16)
            w_out = w_conv_out[j].astype(_BF16)
            gp, tail_p = _conv_in(hp, w_in, w_conv[j], jnp.zeros((CONV_W - 1, d), _F32), None)
            st = state_conv[j]
            inject = jnp.zeros((2, dec_batch, dec_seq, d), _F32)
            inject = inject.at[0, :, 0].set(st[:, 1]).at[1, :, 0].set(st[:, 0]).at[1, :, 1].set(st[:, 1])
            gs, u_s = _conv_in(hs, w_in, w_conv[j], inject.reshape(2, dec_batch * dec_seq, d), dec_seq)
            conv_p.append(tail_p[_SUBLANES - (CONV_W - 1):].reshape(1, CONV_W - 1, d))
            conv_s.append(u_s.reshape(dec_batch, dec_seq, d)[:, dec_seq - (CONV_W - 1):])
            xp = _matmul_res(gp, w_out, xp, 1024, 512, w_resident=True)
            xs = _matmul_res(gs, w_out, xs, 1024, 512, w_resident=True)
        else:
            wqkv = w_qkv[j].astype(_BF16)
            wo = w_o[j].astype(_BF16)
            qp, kp, vtp, kfp, vfp = _qkv(hp, wqkv, g_q[j], g_k[j], attn_chunk)
            qs, ks, vs, kfs, vfs = _qkv(hs, wqkv, g_q[j], g_k[j], None)
            op = _attn_prompt(qp, kp, vtp, attn_tk, heads=4)
            os_ = _attn_sample(qs, ks, vs, cache_k, cache_v, j, attn_tk, attn_chunk, span=2048)
            k_p.append(kfp.reshape(batch, seq, nh, HEAD_DIM))
            v_p.append(vfp.reshape(batch, seq, nh, HEAD_DIM))
            k_s.append(kfs.reshape(dec_batch, dec_seq, nh, HEAD_DIM))
            v_s.append(vfs.reshape(dec_batch, dec_seq, nh, HEAD_DIM))
            xp = _matmul_res(op, wo, xp, 1024, 512, w_resident=True)
            xs = _matmul_res(os_, wo, xs, 1024, 512, w_resident=True)
        wg, wu, wd = w_gate[i].astype(_BF16), w_up[i].astype(_BF16), w_down[i].astype(_BF16)
        xp = _ffn(xp, g_ffn[i], wg, wu, wd)
        xs = _ffn(xs, g_ffn[i], wg, wu, wd)
    return (xp.reshape(batch, seq, d), xs.reshape(dec_batch, dec_seq, d),
            jnp.stack(conv_p), jnp.stack(conv_s), jnp.stack(k_p), jnp.stack(v_p), jnp.stack(k_s), jnp.stack(v_s))
```

```python
import functools

import jax
import jax.numpy as jnp
from jax import lax
from jax.experimental import pallas as pl
from jax.experimental.pallas import tpu as pltpu

EPS = 1e-6
HEAD_DIM = 128
CONV_W = 3
N_MIXERS = 2
SB_SCALE = HEAD_DIM ** -0.5
_LOG2E = 1.4426950408889634
_SIGN_BIT = -2 ** 31
_DEAD_LOG2 = -160.0

_BF16 = jnp.bfloat16
_F32 = jnp.float32
_SUBLANES = 8
_V7X_VMEM_BYTES = 64 * 1024 * 1024
_VMEM_CAP_BYTES = _V7X_VMEM_BYTES - 6 * 1024 * 1024


def _nbytes(shape, dtype):
    n = jnp.dtype(dtype).itemsize
    for s in shape:
        n *= s
    return n


def _params(n_grid, block_bytes, extra_bytes=0, resident_bytes=0):
    est = 2 * block_bytes + resident_bytes + extra_bytes + 4 * 1024 * 1024
    return pltpu.CompilerParams(
        dimension_semantics=("arbitrary",) * n_grid,
        vmem_limit_bytes=int(min(max(est, 16 * 1024 * 1024), _VMEM_CAP_BYTES)),
    )


def _tile(dim, want):
    t = min(dim, want)
    while dim % t:
        t -= 1
    return t


def _dot(a, b):
    return jnp.dot(a, b, preferred_element_type=_F32)


_RESIDENT = dict(pipeline_mode=pl.Buffered(1))


def _cast_on_first(first, w_refs, scr_refs):
    @pl.when(first)
    def _():
        for w_ref, scr in zip(w_refs, scr_refs):
            scr[...] = w_ref[...].astype(scr.dtype)


def _rmsnorm_kernel(x_ref, g_ref, o_ref):
    x = x_ref[...]
    ms = jnp.mean(x * x, axis=-1, keepdims=True)
    o_ref[...] = (x * lax.rsqrt(ms + EPS) * g_ref[...]).astype(o_ref.dtype)


def _rmsnorm(x, g):
    m, d = x.shape
    tm = _tile(m, 256)
    blocks = _nbytes((tm, d), _F32) + _nbytes((tm, d), _BF16)
    return pl.pallas_call(
        _rmsnorm_kernel,
        grid=(m // tm,),
        in_specs=[pl.BlockSpec((tm, d), lambda i: (i, 0)), pl.BlockSpec((1, d), lambda i: (0, 0))],
        out_specs=pl.BlockSpec((tm, d), lambda i: (i, 0)),
        out_shape=jax.ShapeDtypeStruct((m, d), _BF16),
        compiler_params=_params(1, blocks, _nbytes((tm, d), _F32)),
        name="rmsnorm",
    )(x, g.reshape(1, d))


def _conv_in_kernel(h_ref, wb_ref, wc_ref, wx_ref, wconv_ref, st_ref, g_ref, tail_ref, u_scr,
                    wb_scr, wc_scr, wx_scr, *, seg_len, tail_rows):
    i = pl.program_id(1)
    _cast_on_first(i == 0, (wb_ref, wc_ref, wx_ref), (wb_scr, wc_scr, wx_scr))
    hb = h_ref[...]
    b = _dot(hb, wb_scr[...])
    c = _dot(hb, wc_scr[...])
    xin = _dot(hb, wx_scr[...])
    u = c * xin
    tm = u.shape[0]
    u_scr[_SUBLANES:_SUBLANES + tm, :] = u
    if seg_len is None:
        @pl.when(i == 0)
        def _():
            u_scr[0:_SUBLANES - 2, :] = jnp.zeros((_SUBLANES - 2, u.shape[1]), _F32)
            u_scr[_SUBLANES - 2:_SUBLANES, :] = st_ref[...]
        u1 = u_scr[_SUBLANES - 1:_SUBLANES - 1 + tm, :]
        u2 = u_scr[_SUBLANES - 2:_SUBLANES - 2 + tm, :]
    else:
        u_scr[0:_SUBLANES, :] = jnp.zeros((_SUBLANES, u.shape[1]), _F32)
        r = lax.broadcasted_iota(jnp.int32, u.shape, 0) % seg_len
        u1 = jnp.where(r == 0, st_ref[0], u_scr[_SUBLANES - 1:_SUBLANES - 1 + tm, :])
        u2 = jnp.where(r < 2, st_ref[1], u_scr[_SUBLANES - 2:_SUBLANES - 2 + tm, :])
    conv = u2 * wconv_ref[0:1, :] + u1 * wconv_ref[1:2, :] + u * wconv_ref[2:3, :]
    g_ref[...] = (b * conv).astype(g_ref.dtype)
    tail_ref[...] = u[tm - tail_rows:, :]
    if seg_len is None:
        u_scr[0:_SUBLANES, :] = u[tm - _SUBLANES:, :]


def _conv_in(h, w_in, w_conv, state, seg_len):
    m, d = h.shape
    tn = _tile(d, 512)
    nj = d // tn
    if seg_len is None:
        tm = _tile(m, 512)
        tail_rows, tail_shape = _SUBLANES, (_SUBLANES, d)
        st_spec = pl.BlockSpec((2, tn), lambda j, i: (0, j))
        tail_spec = pl.BlockSpec((_SUBLANES, tn), lambda j, i: (0, j))
        st_bytes = _nbytes((2, tn), _F32)
    else:
        tm = m
        tail_rows, tail_shape = m, (m, d)
        st_spec = pl.BlockSpec((2, tm, tn), lambda j, i: (0, 0, j))
        tail_spec = pl.BlockSpec((tm, tn), lambda j, i: (0, j))
        st_bytes = _nbytes((2, tm, tn), _F32)
    blocks = _nbytes((tm, d), _BF16) + st_bytes + _nbytes((tm, tn), _BF16) + _nbytes((tail_rows, tn), _F32)
    scratch = _nbytes((tm + _SUBLANES, tn), _F32) + 3 * _nbytes((d, tn), _BF16)
    return pl.pallas_call(
        functools.partial(_conv_in_kernel, seg_len=seg_len, tail_rows=tail_rows),
        grid=(nj, m // tm),
        in_specs=[
            pl.BlockSpec((tm, d), lambda j, i: (i, 0)),
            pl.BlockSpec((d, tn), lambda j, i: (0, j), **_RESIDENT),
            pl.BlockSpec((d, tn), lambda j, i: (0, j + nj), **_RESIDENT),
            pl.BlockSpec((d, tn), lambda j, i: (0, j + 2 * nj), **_RESIDENT),
            pl.BlockSpec((CONV_W, tn), lambda j, i: (0, j)),
            st_spec,
        ],
        out_specs=[pl.BlockSpec((tm, tn), lambda j, i: (i, j)), tail_spec],
        out_shape=[jax.ShapeDtypeStruct((m, d), _BF16), jax.ShapeDtypeStruct(tail_shape, _F32)],
        scratch_shapes=[pltpu.VMEM((tm + _SUBLANES, tn), _F32)] + [pltpu.VMEM((d, tn), _BF16)] * 3,
        compiler_params=_params(2, blocks, scratch + 6 * _nbytes((tm, tn), _F32), 3 * _nbytes((d, tn), _F32)),
        name="conv_in",
    )(h, w_in, w_in, w_in, w_conv, state)


def _matmul_res_kernel(a_ref, w_ref, r_ref, o_ref):
    o_ref[...] = _dot(a_ref[...], w_ref[...]) + r_ref[...]


def _matmul_res_resident_kernel(a_ref, w_ref, r_ref, o_ref, w_scr):
    _cast_on_first(pl.program_id(1) == 0, (w_ref,), (w_scr,))
    o_ref[...] = _dot(a_ref[...], w_scr[...]) + r_ref[...]


def _matmul_res(a, w, res, tm_want, tn_want):
    m, k = a.shape
    n = w.shape[1]
    tm, tn = _tile(m, tm_want), _tile(n, tn_want)
    out_shape = jax.ShapeDtypeStruct((m, n), _F32)
    io_bytes = _nbytes((tm, k), _BF16) + 2 * _nbytes((tm, tn), _F32)
    if w.dtype == _F32:
        a_map, w_map, o_map = (lambda j, i: (i, 0)), (lambda j, i: (0, j)), (lambda j, i: (i, j))
        return pl.pallas_call(
            _matmul_res_resident_kernel,
            grid=(n // tn, m // tm),
            in_specs=[pl.BlockSpec((tm, k), a_map), pl.BlockSpec((k, tn), w_map, **_RESIDENT),
                      pl.BlockSpec((tm, tn), o_map)],
            out_specs=pl.BlockSpec((tm, tn), o_map),
            out_shape=out_shape,
            scratch_shapes=[pltpu.VMEM((k, tn), _BF16)],
            compiler_params=_params(2, io_bytes, 2 * _nbytes((tm, tn), _F32) + _nbytes((k, tn), _BF16),
                                    _nbytes((k, tn), _F32)),
            name="matmul_res",
        )(a, w, res)
    a_map, w_map, o_map = (lambda i, j: (i, 0)), (lambda i, j: (0, j)), (lambda i, j: (i, j))
    return pl.pallas_call(
        _matmul_res_kernel,
        grid=(m // tm, n // tn),
        in_specs=[pl.BlockSpec((tm, k), a_map), pl.BlockSpec((k, tn), w_map), pl.BlockSpec((tm, tn), o_map)],
        out_specs=pl.BlockSpec((tm, tn), o_map),
        out_shape=out_shape,
        compiler_params=_params(2, io_bytes + _nbytes((k, tn), _BF16), 2 * _nbytes((tm, tn), _F32)),
        name="matmul_res",
    )(a, w, res)


def _ffn_up_kernel(h_ref, wg_ref, wu_ref, o_ref):
    hb = h_ref[...]
    gate = _dot(hb, wg_ref[...].astype(_BF16))
    up = _dot(hb, wu_ref[...].astype(_BF16))
    o_ref[...] = (gate * jax.nn.sigmoid(gate) * up).astype(o_ref.dtype)


def _ffn_up(h, wg, wu):
    m, d = h.shape
    f = wg.shape[1]
    tm, tn = _tile(m, 2048), _tile(f, 256)
    blocks = _nbytes((tm, d), _BF16) + 2 * _nbytes((d, tn), _F32) + _nbytes((tm, tn), _BF16)
    return pl.pallas_call(
        _ffn_up_kernel,
        grid=(m // tm, f // tn),
        in_specs=[
            pl.BlockSpec((tm, d), lambda i, j: (i, 0)),
            pl.BlockSpec((d, tn), lambda i, j: (0, j)),
            pl.BlockSpec((d, tn), lambda i, j: (0, j)),
        ],
        out_specs=pl.BlockSpec((tm, tn), lambda i, j: (i, j)),
        out_shape=jax.ShapeDtypeStruct((m, f), _BF16),
        compiler_params=_params(2, blocks, 4 * _nbytes((tm, tn), _F32) + 2 * _nbytes((d, tn), _BF16)),
        name="ffn_up",
    )(h, wg, wu)


def _qkv_kernel(h_ref, wq_ref, wk_ref, wv_ref, gq_ref, gk_ref, qh_ref, kh_ref, vx_ref, kf_ref, vf_ref,
                wq_scr, wk_scr, wv_scr, *, key_chunk, q_scale):
    _cast_on_first(pl.program_id(1) == 0, (wq_ref, wk_ref, wv_ref), (wq_scr, wk_scr, wv_scr))
    hb = h_ref[...]
    q = _dot(hb, wq_scr[...])
    k = _dot(hb, wk_scr[...])
    v = _dot(hb, wv_scr[...])
    tm = q.shape[0]

    def head_norm(x, g):
        y = x * lax.rsqrt(jnp.mean(x * x, axis=-1, keepdims=True) + EPS)
        return y * g

    for hh in range(q.shape[1] // HEAD_DIM):
        sl = slice(hh * HEAD_DIM, (hh + 1) * HEAD_DIM)
        qh_ref[hh] = (head_norm(q[:, sl], gq_ref[...]) * q_scale).astype(qh_ref.dtype)
        kn = head_norm(k[:, sl], gk_ref[...])
        kf_ref[:, sl] = kn
        kh_ref[hh] = kn.astype(kh_ref.dtype)
        vh = v[:, sl]
        vf_ref[:, sl] = vh
        if key_chunk is None:
            vx_ref[hh] = vh.astype(vx_ref.dtype)
        else:
            vt = vh.T.astype(vx_ref.dtype)
            for cc in range(tm // key_chunk):
                vx_ref[hh, cc] = vt[:, cc * key_chunk:(cc + 1) * key_chunk]


def _qkv(h, w_qkv, g_q, g_k, key_chunk):
    m, d = h.shape
    nh = d // HEAD_DIM
    tn = _tile(d, 512)
    hpt = tn // HEAD_DIM
    nj = d // tn
    tm = _tile(m, 512)
    hm = jax.ShapeDtypeStruct((nh, m, HEAD_DIM), _BF16)
    hm_spec = pl.BlockSpec((hpt, tm, HEAD_DIM), lambda j, i: (j, i, 0))
    if key_chunk is None:
        vx, vx_spec = hm, hm_spec
    else:
        vx = jax.ShapeDtypeStruct((nh, m // key_chunk, HEAD_DIM, key_chunk), _BF16)
        vx_spec = pl.BlockSpec((hpt, tm // key_chunk, HEAD_DIM, key_chunk), lambda j, i: (j, i, 0, 0))
    blocks = _nbytes((tm, d), _BF16) + 3 * _nbytes((tm, tn), _BF16) + 2 * _nbytes((tm, tn), _F32)
    return pl.pallas_call(
        functools.partial(_qkv_kernel, key_chunk=key_chunk, q_scale=SB_SCALE * _LOG2E),
        grid=(nj, m // tm),
        in_specs=[
            pl.BlockSpec((tm, d), lambda j, i: (i, 0)),
            pl.BlockSpec((d, tn), lambda j, i: (0, j), **_RESIDENT),
            pl.BlockSpec((d, tn), lambda j, i: (0, j + nj), **_RESIDENT),
            pl.BlockSpec((d, tn), lambda j, i: (0, j + 2 * nj), **_RESIDENT),
            pl.BlockSpec((1, HEAD_DIM), lambda j, i: (0, 0)),
            pl.BlockSpec((1, HEAD_DIM), lambda j, i: (0, 0)),
        ],
        out_specs=[hm_spec, hm_spec, vx_spec,
                   pl.BlockSpec((tm, tn), lambda j, i: (i, j)), pl.BlockSpec((tm, tn), lambda j, i: (i, j))],
        out_shape=[hm, hm, vx, jax.ShapeDtypeStruct((m, d), _F32), jax.ShapeDtypeStruct((m, d), _F32)],
        scratch_shapes=[pltpu.VMEM((d, tn), _BF16)] * 3,
        compiler_params=_params(2, blocks, 8 * _nbytes((tm, tn), _F32) + 3 * _nbytes((d, tn), _BF16),
                                3 * _nbytes((d, tn), _F32)),
        name="qkv",
    )(h, w_qkv, w_qkv, w_qkv, g_q.reshape(1, HEAD_DIM), g_k.reshape(1, HEAD_DIM))


def _sb_logits(kspan, q, tri, r_ref, masks):
    tk = tri.shape[0]
    z = lax.dot_general(kspan, q, (((1,), (1,)), ((), ())), preferred_element_type=_F32)
    neg_abs = lax.bitcast_convert_type(lax.bitcast_convert_type(z, jnp.int32) | _SIGN_BIT, _F32)
    sp = jnp.maximum(z, 0.0) + jnp.log(1.0 + jnp.exp2(neg_abs)) * _LOG2E
    r = r_ref[0:1, :]
    w_blocks = [None] * len(masks)
    for b in reversed(range(len(masks))):
        zb, spb = z[b * tk:(b + 1) * tk], sp[b * tk:(b + 1) * tk]
        if masks[b]:
            causal = lax.broadcasted_iota(jnp.int32, zb.shape, 0) < lax.broadcasted_iota(jnp.int32, zb.shape, 1)
            spb = jnp.where(causal, spb, 0.0)
        cs = _dot(tri, spb.astype(_BF16))
        wb = jnp.minimum(zb - cs, 0.0) + r
        w_blocks[b] = jnp.where(causal, wb, -jnp.inf) if masks[b] else wb
        r = r - cs[0:1, :]
    r_ref[0:1, :] = r
    return w_blocks[0] if len(masks) == 1 else jnp.concatenate(w_blocks, axis=0)


def _sb_alive(r_ref):
    return jnp.max(r_ref[:, 0, :]) >= _DEAD_LOG2


def _sb_accumulate(w, vt, acc_ref):
    acc_ref[...] += _dot(vt, jnp.exp2(w).astype(_BF16))


def _attn_prompt_kernel(q_ref, k_ref, vt_ref, tri_ref, o_ref, r_ref, acc_ref, w_ref, *, tk, heads, nsub):
    qi = pl.program_id(1)
    tri = tri_ref[...]
    chunk = nsub * tk
    chains = [(h, c) for h in range(heads) for c in range(nsub)]
    r_ref[...] = jnp.zeros_like(r_ref)
    acc_ref[...] = jnp.zeros_like(acc_ref)
    qs = {(h, c): q_ref[h, c * tk:(c + 1) * tk, :] for h, c in chains}

    dstart = pl.multiple_of(qi * chunk, chunk)
    for n, (h, c) in enumerate(chains):
        w = _sb_logits(k_ref[h, pl.ds(dstart, (c + 1) * tk), :], qs[h, c], tri, r_ref.at[n],
                       (False,) * c + (True,))
        if c + 1 < nsub:
            w = jnp.concatenate([w, jnp.full(((nsub - c - 1) * tk, tk), -jnp.inf, _F32)], axis=0)
        w_ref[n] = w

    def logits(kc):
        start = pl.multiple_of(kc * chunk, chunk)
        for n, (h, c) in enumerate(chains):
            w_ref[n] = _sb_logits(k_ref[h, pl.ds(start, chunk), :], qs[h, c], tri, r_ref.at[n], (False,) * nsub)

    def accumulate(kc):
        for n, (h, c) in enumerate(chains):
            _sb_accumulate(w_ref[n], vt_ref[h, kc], acc_ref.at[n])

    def body(carry):
        t, _ = carry
        accumulate(qi - t)
        logits(qi - 1 - t)
        return t + 1, _sb_alive(r_ref)

    t, _ = lax.while_loop(lambda carry: (carry[0] < qi) & carry[1], body, (jnp.int32(0), _sb_alive(r_ref)))
    accumulate(qi - t)
    for n, (h, c) in enumerate(chains):
        o_ref[c * tk:(c + 1) * tk, h * HEAD_DIM:(h + 1) * HEAD_DIM] = acc_ref[n].T.astype(o_ref.dtype)


def _tri(tk):
    idx = jnp.arange(tk)
    return (idx[None, :] >= idx[:, None]).astype(_BF16)


def _attn_prompt(q_hm, k_hm, vt, tk, heads):
    nh, m, hd = q_hm.shape
    chunk = vt.shape[3]
    nsub = chunk // tk
    blocks = 2 * _nbytes((heads, chunk, hd), _BF16) + _nbytes((tk, tk), _BF16)
    return pl.pallas_call(
        functools.partial(_attn_prompt_kernel, tk=tk, heads=heads, nsub=nsub),
        grid=(nh // heads, m // chunk),
        in_specs=[
            pl.BlockSpec((heads, chunk, hd), lambda g, i: (g, i, 0)),
            pl.BlockSpec((heads, m, hd), lambda g, i: (g, 0, 0), **_RESIDENT),
            pl.BlockSpec((heads, m // chunk, hd, chunk), lambda g, i: (g, 0, 0, 0), **_RESIDENT),
            pl.BlockSpec((tk, tk), lambda g, i: (0, 0)),
        ],
        out_specs=pl.BlockSpec((chunk, heads * hd), lambda g, i: (i, g)),
        out_shape=jax.ShapeDtypeStruct((m, nh * hd), _BF16),
        scratch_shapes=[pltpu.VMEM((heads * nsub, _SUBLANES, tk), _F32), pltpu.VMEM((heads * nsub, hd, tk), _F32),
                        pltpu.VMEM((heads * nsub, chunk, tk), _F32)],
        compiler_params=_params(2, blocks, 17 * heads * nsub * _nbytes((chunk, tk), _F32),
                                2 * _nbytes((heads, m, hd), _BF16)),
        name="sb_attn_prompt",
    )(q_hm, k_hm, vt, _tri(tk))


def _attn_sample_kernel(q_ref, kn_ref, vn_ref, ck_ref, cv_ref, tri_ref, o_ref, r_ref, acc_ref, w_ref,
                        *, tk, tq, nsub):
    step = pl.program_id(2)
    heads, t, hd = q_ref.shape
    chunk = nsub * tk
    n_chunks = ck_ref.shape[0] // chunk
    tri = tri_ref[...]
    qs = [jnp.concatenate([q_ref[h], jnp.zeros((tq - t, hd), _BF16)], axis=0) for h in range(heads)]

    @pl.when(step == 0)
    def _():
        r_ref[...] = jnp.zeros_like(r_ref)
        acc_ref[...] = jnp.zeros_like(acc_ref)
        for h in range(heads):
            k_new = jnp.concatenate([kn_ref[h], jnp.zeros((tk - t, hd), _BF16)], axis=0)
            v_new = jnp.concatenate([vn_ref[h].astype(_F32), jnp.zeros((tk - t, hd), _F32)], axis=0)
            w = _sb_logits(k_new, qs[h], tri, r_ref.at[h], (True,))
            _sb_accumulate(w, v_new.T.astype(_BF16), acc_ref.at[h])

    def logits(kc):
        start = pl.multiple_of(kc * chunk, chunk)
        for h in range(heads):
            kspan = ck_ref[pl.ds(start, chunk), h, :].astype(_BF16)
            w_ref[h] = _sb_logits(kspan, qs[h], tri, r_ref.at[h], (False,) * nsub)

    def accumulate(kc):
        start = pl.multiple_of(kc * chunk, chunk)
        for h in range(heads):
            vt = cv_ref[pl.ds(start, chunk), h, :].T.astype(_BF16)
            _sb_accumulate(w_ref[h], vt, acc_ref.at[h])

    @pl.when(_sb_alive(r_ref))
    def _():
        logits(n_chunks - 1)

        def body(carry):
            i, _ = carry
            accumulate(n_chunks - 1 - i)
            logits(n_chunks - 2 - i)
            return i + 1, _sb_alive(r_ref)

        i, _ = lax.while_loop(lambda carry: (carry[0] < n_chunks - 1) & carry[1], body,
                              (jnp.int32(0), _sb_alive(r_ref)))
        accumulate(n_chunks - 1 - i)

    @pl.when(step == pl.num_programs(2) - 1)
    def _():
        for h in range(heads):
            o_ref[:, h * hd:(h + 1) * hd] = acc_ref[h].T[0:t, :].astype(o_ref.dtype)


def _attn_sample(q_hm, k_hm, v_hm, cache_k, cache_v, layer, tk, chunk, span):
    nh, m, hd = q_hm.shape
    n_streams, past = cache_k.shape[1], cache_k.shape[2]
    t = m // n_streams
    heads = _SUBLANES
    assert past % span == 0 and span % chunk == 0 and nh % heads == 0 and t <= HEAD_DIM
    tq = HEAD_DIM
    n_spans = past // span
    grouped = (cache_k.shape[0], n_streams, past, nh // heads, heads, hd)
    new_spec = pl.BlockSpec((heads, t, hd), lambda s, g, c: (g, s, 0))
    cache_spec = pl.BlockSpec((None, None, span, None, heads, hd),
                              lambda s, g, c: (layer, s, n_spans - 1 - c, g, 0, 0))
    blocks = (3 * _nbytes((heads, t, hd), _BF16) + 2 * _nbytes((span, heads, hd), _F32)
              + _nbytes((tk, tk), _BF16))
    return pl.pallas_call(
        functools.partial(_attn_sample_kernel, tk=tk, tq=tq, nsub=chunk // tk),
        grid=(n_streams, nh // heads, n_spans),
        in_specs=[new_spec, new_spec, new_spec, cache_spec, cache_spec,
                  pl.BlockSpec((tk, tk), lambda s, g, c: (0, 0))],
        out_specs=pl.BlockSpec((t, heads * hd), lambda s, g, c: (s, g)),
        out_shape=jax.ShapeDtypeStruct((m, nh * hd), _BF16),
        scratch_shapes=[pltpu.VMEM((heads, _SUBLANES, tq), _F32), pltpu.VMEM((heads, hd, tq), _F32),
                        pltpu.VMEM((heads, chunk, tq), _F32)],
        compiler_params=_params(3, blocks, 17 * heads * _nbytes((chunk, tq), _F32)),
        name="sb_attn_sample",
    )(q_hm, k_hm, v_hm, cache_k.reshape(grouped), cache_v.reshape(grouped), _tri(tk))


def _ffn(x, g, wg, wu, wd):
    mid = _ffn_up(_rmsnorm(x, g), wg, wu)
    return _matmul_res(mid, wd, x, 512, 512)


def kernel(x_prompt, x_sample, state_conv, cache_k, cache_v, g_mix, g_ffn, w_conv_in, w_conv, w_conv_out,
           w_qkv, g_q, g_k, w_o, w_gate, w_up, w_down):
    batch, seq, d = x_prompt.shape
    dec_batch, dec_seq, _ = x_sample.shape
    assert batch == 1, "the prompt path handles one fresh stream"
    nh = d // HEAD_DIM
    depth = g_mix.shape[0]
    attn_tk, attn_chunk = 256, 512
    xp = x_prompt.reshape(seq, d)
    xs = x_sample.reshape(dec_batch * dec_seq, d)
    conv_p, conv_s, k_p, v_p, k_s, v_s = [], [], [], [], [], []
    for i in range(depth):
        j = i // N_MIXERS
        hp = _rmsnorm(xp, g_mix[i])
        hs = _rmsnorm(xs, g_mix[i])
        if i % N_MIXERS == 0:
            w_in, w_out = w_conv_in[j], w_conv_out[j]
            gp, tail_p = _conv_in(hp, w_in, w_conv[j], jnp.zeros((CONV_W - 1, d), _F32), None)
            st = state_conv[j]
            inject = jnp.zeros((2, dec_batch, dec_seq, d), _F32)
            inject = inject.at[0, :, 0].set(st[:, 1]).at[1, :, 0].set(st[:, 0]).at[1, :, 1].set(st[:, 1])
            gs, u_s = _conv_in(hs, w_in, w_conv[j], inject.reshape(2, dec_batch * dec_seq, d), dec_seq)
            conv_p.append(tail_p[_SUBLANES - (CONV_W - 1):].reshape(1, CONV_W - 1, d))
            conv_s.append(u_s.reshape(dec_batch, dec_seq, d)[:, dec_seq - (CONV_W - 1):])
            xp = _matmul_res(gp, w_out, xp, 1024, 512)
            xs = _matmul_res(gs, w_out, xs, 1024, 512)
        else:
            wqkv, wo = w_qkv[j], w_o[j]
            qp, kp, vtp, kfp, vfp = _qkv(hp, wqkv, g_q[j], g_k[j], attn_chunk)
            qs, ks, vs, kfs, vfs = _qkv(hs, wqkv, g_q[j], g_k[j], None)
            op = _attn_prompt(qp, kp, vtp, attn_tk, heads=4)
            os_ = _attn_sample(qs, ks, vs, cache_k, cache_v, j, attn_tk, attn_chunk, span=2048)
            k_p.append(kfp.reshape(batch, seq, nh, HEAD_DIM))
            v_p.append(vfp.reshape(batch, seq, nh, HEAD_DIM))
            k_s.append(kfs.reshape(dec_batch, dec_seq, nh, HEAD_DIM))
            v_s.append(vfs.reshape(dec_batch, dec_seq, nh, HEAD_DIM))
            xp = _matmul_res(op, wo, xp, 1024, 512)
            xs = _matmul_res(os_, wo, xs, 1024, 512)
        wg, wu, wd = w_gate[i], w_up[i], w_down[i].astype(_BF16)
        xp = _ffn(xp, g_ffn[i], wg, wu, wd)
        xs = _ffn(xs, g_ffn[i], wg, wu, wd)
    return (xp.reshape(batch, seq, d), xs.reshape(dec_batch, dec_seq, d),
            jnp.stack(conv_p), jnp.stack(conv_s), jnp.stack(k_p), jnp.stack(v_p), jnp.stack(k_s), jnp.stack(v_s))
```

```python
import functools

import jax
import jax.numpy as jnp
from jax import lax
from jax.experimental import pallas as pl
from jax.experimental.pallas import tpu as pltpu

EPS = 1e-6
HEAD_DIM = 128
CONV_W = 3
N_MIXERS = 2
SB_SCALE = HEAD_DIM ** -0.5
_LOG2E = 1.4426950408889634
_SIGN_BIT = -2 ** 31
_DEAD_LOG2 = -160.0

_BF16 = jnp.bfloat16
_F32 = jnp.float32
_SUBLANES = 8
_V7X_VMEM_BYTES = 64 * 1024 * 1024
_VMEM_CAP_BYTES = _V7X_VMEM_BYTES - 6 * 1024 * 1024


def _nbytes(shape, dtype):
    n = jnp.dtype(dtype).itemsize
    for s in shape:
        n *= s
    return n


def _params(n_grid, block_bytes, extra_bytes=0, resident_bytes=0):
    est = 2 * block_bytes + resident_bytes + extra_bytes + 4 * 1024 * 1024
    return pltpu.CompilerParams(
        dimension_semantics=("arbitrary",) * n_grid,
        vmem_limit_bytes=int(min(max(est, 16 * 1024 * 1024), _VMEM_CAP_BYTES)),
    )


def _tile(dim, want):
    t = min(dim, want)
    while dim % t:
        t -= 1
    return t


def _dot(a, b):
    return jnp.dot(a, b, preferred_element_type=_F32)


_RESIDENT = dict(pipeline_mode=pl.Buffered(1))


def _weight_tiling(m, tm, d):
    if m // tm > 1:
        return _tile(d, 512), _RESIDENT, 1
    return _tile(d, 256), {}, 2


def _cast_on_first(first, w_refs, scr_refs):
    @pl.when(first)
    def _():
        for w_ref, scr in zip(w_refs, scr_refs):
            scr[...] = w_ref[...].astype(scr.dtype)


def _rmsnorm_kernel(x_ref, g_ref, o_ref):
    x = x_ref[...]
    ms = jnp.mean(x * x, axis=-1, keepdims=True)
    o_ref[...] = (x * lax.rsqrt(ms + EPS) * g_ref[...]).astype(o_ref.dtype)


def _rmsnorm(x, g):
    m, d = x.shape
    tm = _tile(m, 256)
    blocks = _nbytes((tm, d), _F32) + _nbytes((tm, d), _BF16)
    return pl.pallas_call(
        _rmsnorm_kernel,
        grid=(m // tm,),
        in_specs=[pl.BlockSpec((tm, d), lambda i: (i, 0)), pl.BlockSpec((1, d), lambda i: (0, 0))],
        out_specs=pl.BlockSpec((tm, d), lambda i: (i, 0)),
        out_shape=jax.ShapeDtypeStruct((m, d), _BF16),
        compiler_params=_params(1, blocks, _nbytes((tm, d), _F32)),
        name="rmsnorm",
    )(x, g.reshape(1, d))


def _conv_in_kernel(h_ref, wb_ref, wc_ref, wx_ref, wconv_ref, st_ref, g_ref, tail_ref, u_scr,
                    wb_scr, wc_scr, wx_scr, *, seg_len, tail_rows):
    i = pl.program_id(1)
    _cast_on_first(i == 0, (wb_ref, wc_ref, wx_ref), (wb_scr, wc_scr, wx_scr))
    hb = h_ref[...]
    b = _dot(hb, wb_scr[...])
    c = _dot(hb, wc_scr[...])
    xin = _dot(hb, wx_scr[...])
    u = c * xin
    tm = u.shape[0]
    u_scr[_SUBLANES:_SUBLANES + tm, :] = u
    if seg_len is None:
        @pl.when(i == 0)
        def _():
            u_scr[0:_SUBLANES - 2, :] = jnp.zeros((_SUBLANES - 2, u.shape[1]), _F32)
            u_scr[_SUBLANES - 2:_SUBLANES, :] = st_ref[...]
        u1 = u_scr[_SUBLANES - 1:_SUBLANES - 1 + tm, :]
        u2 = u_scr[_SUBLANES - 2:_SUBLANES - 2 + tm, :]
    else:
        u_scr[0:_SUBLANES, :] = jnp.zeros((_SUBLANES, u.shape[1]), _F32)
        r = lax.broadcasted_iota(jnp.int32, u.shape, 0) % seg_len
        u1 = jnp.where(r == 0, st_ref[0], u_scr[_SUBLANES - 1:_SUBLANES - 1 + tm, :])
        u2 = jnp.where(r < 2, st_ref[1], u_scr[_SUBLANES - 2:_SUBLANES - 2 + tm, :])
    conv = u2 * wconv_ref[0:1, :] + u1 * wconv_ref[1:2, :] + u * wconv_ref[2:3, :]
    g_ref[...] = (b * conv).astype(g_ref.dtype)
    tail_ref[...] = u[tm - tail_rows:, :]
    if seg_len is None:
        u_scr[0:_SUBLANES, :] = u[tm - _SUBLANES:, :]


def _conv_in(h, w_in, layer, w_conv, state, seg_len):
    m, d = h.shape
    tm = _tile(m, 512) if seg_len is None else m
    tn, w_mode, w_bufs = _weight_tiling(m, tm, d)
    nj = d // tn
    if seg_len is None:
        tail_rows, tail_shape = _SUBLANES, (_SUBLANES, d)
        st_spec = pl.BlockSpec((2, tn), lambda j, i: (0, j))
        tail_spec = pl.BlockSpec((_SUBLANES, tn), lambda j, i: (0, j))
        st_bytes = _nbytes((2, tn), _F32)
    else:
        tail_rows, tail_shape = m, (m, d)
        st_spec = pl.BlockSpec((2, tm, tn), lambda j, i: (0, 0, j))
        tail_spec = pl.BlockSpec((tm, tn), lambda j, i: (0, j))
        st_bytes = _nbytes((2, tm, tn), _F32)
    blocks = _nbytes((tm, d), _BF16) + st_bytes + _nbytes((tm, tn), _BF16) + _nbytes((tail_rows, tn), _F32)
    scratch = _nbytes((tm + _SUBLANES, tn), _F32) + 3 * _nbytes((d, tn), _BF16)
    return pl.pallas_call(
        functools.partial(_conv_in_kernel, seg_len=seg_len, tail_rows=tail_rows),
        grid=(nj, m // tm),
        in_specs=[
            pl.BlockSpec((tm, d), lambda j, i: (i, 0)),
            pl.BlockSpec((None, d, tn), lambda j, i: (layer, 0, j), **w_mode),
            pl.BlockSpec((None, d, tn), lambda j, i: (layer, 0, j + nj), **w_mode),
            pl.BlockSpec((None, d, tn), lambda j, i: (layer, 0, j + 2 * nj), **w_mode),
            pl.BlockSpec((CONV_W, tn), lambda j, i: (0, j)),
            st_spec,
        ],
        out_specs=[pl.BlockSpec((tm, tn), lambda j, i: (i, j)), tail_spec],
        out_shape=[jax.ShapeDtypeStruct((m, d), _BF16), jax.ShapeDtypeStruct(tail_shape, _F32)],
        scratch_shapes=[pltpu.VMEM((tm + _SUBLANES, tn), _F32)] + [pltpu.VMEM((d, tn), _BF16)] * 3,
        compiler_params=_params(2, blocks, scratch + 6 * _nbytes((tm, tn), _F32),
                                3 * w_bufs * _nbytes((d, tn), _F32)),
        name="conv_in",
    )(h, w_in, w_in, w_in, w_conv, state)


def _matmul_res_kernel(a_ref, w_ref, r_ref, o_ref):
    o_ref[...] = _dot(a_ref[...], w_ref[...]) + r_ref[...]


def _matmul_res_resident_kernel(a_ref, w_ref, r_ref, o_ref, w_scr):
    _cast_on_first(pl.program_id(1) == 0, (w_ref,), (w_scr,))
    o_ref[...] = _dot(a_ref[...], w_scr[...]) + r_ref[...]


def _matmul_res(a, w, layer, res, tm_want, tn_want):
    m, k = a.shape
    n = w.shape[2]
    tm, tn = _tile(m, tm_want), _tile(n, tn_want)
    out_shape = jax.ShapeDtypeStruct((m, n), _F32)
    io_bytes = _nbytes((tm, k), _BF16) + 2 * _nbytes((tm, tn), _F32)
    if w.dtype == _F32:
        a_map, w_map, o_map = (lambda j, i: (i, 0)), (lambda j, i: (layer, 0, j)), (lambda j, i: (i, j))
        w_mode, w_bufs = (_RESIDENT, 1) if m // tm > 1 else ({}, 2)
        return pl.pallas_call(
            _matmul_res_resident_kernel,
            grid=(n // tn, m // tm),
            in_specs=[pl.BlockSpec((tm, k), a_map), pl.BlockSpec((None, k, tn), w_map, **w_mode),
                      pl.BlockSpec((tm, tn), o_map)],
            out_specs=pl.BlockSpec((tm, tn), o_map),
            out_shape=out_shape,
            scratch_shapes=[pltpu.VMEM((k, tn), _BF16)],
            compiler_params=_params(2, io_bytes, 2 * _nbytes((tm, tn), _F32) + _nbytes((k, tn), _BF16),
                                    w_bufs * _nbytes((k, tn), _F32)),
            name="matmul_res",
        )(a, w, res)
    a_map, w_map, o_map = (lambda i, j: (i, 0)), (lambda i, j: (layer, 0, j)), (lambda i, j: (i, j))
    return pl.pallas_call(
        _matmul_res_kernel,
        grid=(m // tm, n // tn),
        in_specs=[pl.BlockSpec((tm, k), a_map), pl.BlockSpec((None, k, tn), w_map), pl.BlockSpec((tm, tn), o_map)],
        out_specs=pl.BlockSpec((tm, tn), o_map),
        out_shape=out_shape,
        compiler_params=_params(2, io_bytes + _nbytes((k, tn), _BF16), 2 * _nbytes((tm, tn), _F32)),
        name="matmul_res",
    )(a, w, res)


def _ffn_up_kernel(h_ref, wg_ref, wu_ref, o_ref):
    hb = h_ref[...]
    gate = _dot(hb, wg_ref[...].astype(_BF16))
    up = _dot(hb, wu_ref[...].astype(_BF16))
    o_ref[...] = (gate * jax.nn.sigmoid(gate) * up).astype(o_ref.dtype)


def _ffn_up(h, wg, wu, layer):
    m, d = h.shape
    f = wg.shape[2]
    tm, tn = _tile(m, 2048), _tile(f, 256)
    blocks = _nbytes((tm, d), _BF16) + 2 * _nbytes((d, tn), _F32) + _nbytes((tm, tn), _BF16)
    return pl.pallas_call(
        _ffn_up_kernel,
        grid=(m // tm, f // tn),
        in_specs=[
            pl.BlockSpec((tm, d), lambda i, j: (i, 0)),
            pl.BlockSpec((None, d, tn), lambda i, j: (layer, 0, j)),
            pl.BlockSpec((None, d, tn), lambda i, j: (layer, 0, j)),
        ],
        out_specs=pl.BlockSpec((tm, tn), lambda i, j: (i, j)),
        out_shape=jax.ShapeDtypeStruct((m, f), _BF16),
        compiler_params=_params(2, blocks, 4 * _nbytes((tm, tn), _F32) + 2 * _nbytes((d, tn), _BF16)),
        name="ffn_up",
    )(h, wg, wu)


def _qkv_kernel(h_ref, wq_ref, wk_ref, wv_ref, gq_ref, gk_ref, qh_ref, kh_ref, vx_ref, kf_ref, vf_ref,
                wq_scr, wk_scr, wv_scr, *, key_chunk, q_scale):
    _cast_on_first(pl.program_id(1) == 0, (wq_ref, wk_ref, wv_ref), (wq_scr, wk_scr, wv_scr))
    hb = h_ref[...]
    q = _dot(hb, wq_scr[...])
    k = _dot(hb, wk_scr[...])
    v = _dot(hb, wv_scr[...])
    tm = q.shape[0]

    def head_norm(x, g):
        y = x * lax.rsqrt(jnp.mean(x * x, axis=-1, keepdims=True) + EPS)
        return y * g

    for hh in range(q.shape[1] // HEAD_DIM):
        sl = slice(hh * HEAD_DIM, (hh + 1) * HEAD_DIM)
        qh_ref[hh] = (head_norm(q[:, sl], gq_ref[...]) * q_scale).astype(qh_ref.dtype)
        kn = head_norm(k[:, sl], gk_ref[...])
        kf_ref[:, sl] = kn
        kh_ref[hh] = kn.astype(kh_ref.dtype)
        vh = v[:, sl]
        vf_ref[:, sl] = vh
        if key_chunk is None:
            vx_ref[hh] = vh.astype(vx_ref.dtype)
        else:
            vt = vh.T.astype(vx_ref.dtype)
            for cc in range(tm // key_chunk):
                vx_ref[hh, cc] = vt[:, cc * key_chunk:(cc + 1) * key_chunk]


def _qkv(h, w_qkv, layer, g_q, g_k, key_chunk):
    m, d = h.shape
    nh = d // HEAD_DIM
    tm = _tile(m, 512)
    tn, w_mode, w_bufs = _weight_tiling(m, tm, d)
    hpt = tn // HEAD_DIM
    nj = d // tn
    hm = jax.ShapeDtypeStruct((nh, m, HEAD_DIM), _BF16)
    hm_spec = pl.BlockSpec((hpt, tm, HEAD_DIM), lambda j, i: (j, i, 0))
    if key_chunk is None:
        vx, vx_spec = hm, hm_spec
    else:
        vx = jax.ShapeDtypeStruct((nh, m // key_chunk, HEAD_DIM, key_chunk), _BF16)
        vx_spec = pl.BlockSpec((hpt, tm // key_chunk, HEAD_DIM, key_chunk), lambda j, i: (j, i, 0, 0))
    blocks = _nbytes((tm, d), _BF16) + 3 * _nbytes((tm, tn), _BF16) + 2 * _nbytes((tm, tn), _F32)
    return pl.pallas_call(
        functools.partial(_qkv_kernel, key_chunk=key_chunk, q_scale=SB_SCALE * _LOG2E),
        grid=(nj, m // tm),
        in_specs=[
            pl.BlockSpec((tm, d), lambda j, i: (i, 0)),
            pl.BlockSpec((None, d, tn), lambda j, i: (layer, 0, j), **w_mode),
            pl.BlockSpec((None, d, tn), lambda j, i: (layer, 0, j + nj), **w_mode),
            pl.BlockSpec((None, d, tn), lambda j, i: (layer, 0, j + 2 * nj), **w_mode),
            pl.BlockSpec((1, HEAD_DIM), lambda j, i: (0, 0)),
            pl.BlockSpec((1, HEAD_DIM), lambda j, i: (0, 0)),
        ],
        out_specs=[hm_spec, hm_spec, vx_spec,
                   pl.BlockSpec((tm, tn), lambda j, i: (i, j)), pl.BlockSpec((tm, tn), lambda j, i: (i, j))],
        out_shape=[hm, hm, vx, jax.ShapeDtypeStruct((m, d), _F32), jax.ShapeDtypeStruct((m, d), _F32)],
        scratch_shapes=[pltpu.VMEM((d, tn), _BF16)] * 3,
        compiler_params=_params(2, blocks, 8 * _nbytes((tm, tn), _F32) + 3 * _nbytes((d, tn), _BF16),
                                3 * w_bufs * _nbytes((d, tn), _F32)),
        name="qkv",
    )(h, w_qkv, w_qkv, w_qkv, g_q.reshape(1, HEAD_DIM), g_k.reshape(1, HEAD_DIM))


def _sb_logits(kspan, q, tri, r_ref, masks):
    tk = tri.shape[0]
    z = lax.dot_general(kspan, q, (((1,), (1,)), ((), ())), preferred_element_type=_F32)
    neg_abs = lax.bitcast_convert_type(lax.bitcast_convert_type(z, jnp.int32) | _SIGN_BIT, _F32)
    sp = jnp.maximum(z, 0.0) + jnp.log(1.0 + jnp.exp2(neg_abs)) * _LOG2E
    r = r_ref[0:1, :]
    w_blocks = [None] * len(masks)
    for b in reversed(range(len(masks))):
        zb, spb = z[b * tk:(b + 1) * tk], sp[b * tk:(b + 1) * tk]
        if masks[b]:
            causal = lax.broadcasted_iota(jnp.int32, zb.shape, 0) < lax.broadcasted_iota(jnp.int32, zb.shape, 1)
            spb = jnp.where(causal, spb, 0.0)
        cs = _dot(tri, spb.astype(_BF16))
        wb = jnp.minimum(zb - cs, 0.0) + r
        w_blocks[b] = jnp.where(causal, wb, -jnp.inf) if masks[b] else wb
        r = r - cs[0:1, :]
    r_ref[0:1, :] = r
    return w_blocks[0] if len(masks) == 1 else jnp.concatenate(w_blocks, axis=0)


def _sb_alive(r_ref):
    return jnp.max(r_ref[:, 0, :]) >= _DEAD_LOG2


def _sb_accumulate(w, vt, acc_ref):
    acc_ref[...] += _dot(vt, jnp.exp2(w).astype(_BF16))


def _attn_prompt_kernel(q_ref, k_ref, vt_ref, tri_ref, o_ref, r_ref, acc_ref, w_ref, *, tk, heads, nsub):
    qi = pl.program_id(1)
    tri = tri_ref[...]
    chunk = nsub * tk
    chains = [(h, c) for h in range(heads) for c in range(nsub)]
    r_ref[...] = jnp.zeros_like(r_ref)
    acc_ref[...] = jnp.zeros_like(acc_ref)
    qs = {(h, c): q_ref[h, c * tk:(c + 1) * tk, :] for h, c in chains}

    dstart = pl.multiple_of(qi * chunk, chunk)
    for n, (h, c) in enumerate(chains):
        w = _sb_logits(k_ref[h, pl.ds(dstart, (c + 1) * tk), :], qs[h, c], tri, r_ref.at[n],
                       (False,) * c + (True,))
        if c + 1 < nsub:
            w = jnp.concatenate([w, jnp.full(((nsub - c - 1) * tk, tk), -jnp.inf, _F32)], axis=0)
        w_ref[n] = w

    def logits(kc):
        start = pl.multiple_of(kc * chunk, chunk)
        for n, (h, c) in enumerate(chains):
            w_ref[n] = _sb_logits(k_ref[h, pl.ds(start, chunk), :], qs[h, c], tri, r_ref.at[n], (False,) * nsub)

    def accumulate(kc):
        for n, (h, c) in enumerate(chains):
            _sb_accumulate(w_ref[n], vt_ref[h, kc], acc_ref.at[n])

    def body(carry):
        t, _ = carry
        accumulate(qi - t)
        logits(qi - 1 - t)
        return t + 1, _sb_alive(r_ref)

    t, _ = lax.while_loop(lambda carry: (carry[0] < qi) & carry[1], body, (jnp.int32(0), _sb_alive(r_ref)))
    accumulate(qi - t)
    for n, (h, c) in enumerate(chains):
        o_ref[c * tk:(c + 1) * tk, h * HEAD_DIM:(h + 1) * HEAD_DIM] = acc_ref[n].T.astype(o_ref.dtype)


def _tri(tk):
    idx = jnp.arange(tk)
    return (idx[None, :] >= idx[:, None]).astype(_BF16)


def _attn_prompt(q_hm, k_hm, vt, tk, heads):
    nh, m, hd = q_hm.shape
    chunk = vt.shape[3]
    nsub = chunk // tk
    blocks = 2 * _nbytes((heads, chunk, hd), _BF16) + _nbytes((tk, tk), _BF16)
    return pl.pallas_call(
        functools.partial(_attn_prompt_kernel, tk=tk, heads=heads, nsub=nsub),
        grid=(nh // heads, m // chunk),
        in_specs=[
            pl.BlockSpec((heads, chunk, hd), lambda g, i: (g, i, 0)),
            pl.BlockSpec((heads, m, hd), lambda g, i: (g, 0, 0), **_RESIDENT),
            pl.BlockSpec((heads, m // chunk, hd, chunk), lambda g, i: (g, 0, 0, 0), **_RESIDENT),
            pl.BlockSpec((tk, tk), lambda g, i: (0, 0)),
        ],
        out_specs=pl.BlockSpec((chunk, heads * hd), lambda g, i: (i, g)),
        out_shape=jax.ShapeDtypeStruct((m, nh * hd), _BF16),
        scratch_shapes=[pltpu.VMEM((heads * nsub, _SUBLANES, tk), _F32), pltpu.VMEM((heads * nsub, hd, tk), _F32),
                        pltpu.VMEM((heads * nsub, chunk, tk), _F32)],
        compiler_params=_params(2, blocks, 17 * heads * nsub * _nbytes((chunk, tk), _F32),
                                2 * _nbytes((heads, m, hd), _BF16)),
        name="sb_attn_prompt",
    )(q_hm, k_hm, vt, _tri(tk))


def _attn_sample_kernel(q_ref, kn_ref, vn_ref, ck_ref, cv_ref, tri_ref, o_ref, r_ref, acc_ref, w_ref,
                        *, tk, tq, nsub):
    step = pl.program_id(2)
    heads, t, hd = q_ref.shape
    chunk = nsub * tk
    n_chunks = ck_ref.shape[0] // chunk
    tri = tri_ref[...]
    qs = [jnp.concatenate([q_ref[h], jnp.zeros((tq - t, hd), _BF16)], axis=0) for h in range(heads)]

    @pl.when(step == 0)
    def _():
        r_ref[...] = jnp.zeros_like(r_ref)
        acc_ref[...] = jnp.zeros_like(acc_ref)
        for h in range(heads):
            k_new = jnp.concatenate([kn_ref[h], jnp.zeros((tk - t, hd), _BF16)], axis=0)
            v_new = jnp.concatenate([vn_ref[h].astype(_F32), jnp.zeros((tk - t, hd), _F32)], axis=0)
            w = _sb_logits(k_new, qs[h], tri, r_ref.at[h], (True,))
            _sb_accumulate(w, v_new.T.astype(_BF16), acc_ref.at[h])

    def logits(kc):
        start = pl.multiple_of(kc * chunk, chunk)
        for h in range(heads):
            kspan = ck_ref[pl.ds(start, chunk), h, :].astype(_BF16)
            w_ref[h] = _sb_logits(kspan, qs[h], tri, r_ref.at[h], (False,) * nsub)

    def accumulate(kc):
        start = pl.multiple_of(kc * chunk, chunk)
        for h in range(heads):
            vt = cv_ref[pl.ds(start, chunk), h, :].T.astype(_BF16)
            _sb_accumulate(w_ref[h], vt, acc_ref.at[h])

    @pl.when(_sb_alive(r_ref))
    def _():
        logits(n_chunks - 1)

        def body(carry):
            i, _ = carry
            accumulate(n_chunks - 1 - i)
            logits(n_chunks - 2 - i)
            return i + 1, _sb_alive(r_ref)

        i, _ = lax.while_loop(lambda carry: (carry[0] < n_chunks - 1) & carry[1], body,
                              (jnp.int32(0), _sb_alive(r_ref)))
        accumulate(n_chunks - 1 - i)

    @pl.when(step == pl.num_programs(2) - 1)
    def _():
        for h in range(heads):
            o_ref[:, h * hd:(h + 1) * hd] = acc_ref[h].T[0:t, :].astype(o_ref.dtype)


def _attn_sample(q_hm, k_hm, v_hm, cache_k, cache_v, layer, tk, chunk, span):
    nh, m, hd = q_hm.shape
    n_streams, past = cache_k.shape[1], cache_k.shape[2]
    t = m // n_streams
    heads = _SUBLANES
    assert past % span == 0 and span % chunk == 0 and nh % heads == 0 and t <= HEAD_DIM
    tq = HEAD_DIM
    n_spans = past // span
    grouped = (cache_k.shape[0], n_streams, past, nh // heads, heads, hd)
    new_spec = pl.BlockSpec((heads, t, hd), lambda s, g, c: (g, s, 0))
    cache_spec = pl.BlockSpec((None, None, span, None, heads, hd),
                              lambda s, g, c: (layer, s, n_spans - 1 - c, g, 0, 0))
    blocks = (3 * _nbytes((heads, t, hd), _BF16) + 2 * _nbytes((span, heads, hd), _F32)
              + _nbytes((tk, tk), _BF16))
    return pl.pallas_call(
        functools.partial(_attn_sample_kernel, tk=tk, tq=tq, nsub=chunk // tk),
        grid=(n_streams, nh // heads, n_spans),
        in_specs=[new_spec, new_spec, new_spec, cache_spec, cache_spec,
                  pl.BlockSpec((tk, tk), lambda s, g, c: (0, 0))],
        out_specs=pl.BlockSpec((t, heads * hd), lambda s, g, c: (s, g)),
        out_shape=jax.ShapeDtypeStruct((m, nh * hd), _BF16),
        scratch_shapes=[pltpu.VMEM((heads, _SUBLANES, tq), _F32), pltpu.VMEM((heads, hd, tq), _F32),
                        pltpu.VMEM((heads, chunk, tq), _F32)],
        compiler_params=_params(3, blocks, 17 * heads * _nbytes((chunk, tq), _F32)),
        name="sb_attn_sample",
    )(q_hm, k_hm, v_hm, cache_k.reshape(grouped), cache_v.reshape(grouped), _tri(tk))


def _ffn(x, g, wg, wu, wd, layer):
    mid = _ffn_up(_rmsnorm(x, g), wg, wu, layer)
    return _matmul_res(mid, wd, layer, x, 512, 512)


def kernel(x_prompt, x_sample, state_conv, cache_k, cache_v, g_mix, g_ffn, w_conv_in, w_conv, w_conv_out,
           w_qkv, g_q, g_k, w_o, w_gate, w_up, w_down):
    batch, seq, d = x_prompt.shape
    dec_batch, dec_seq, _ = x_sample.shape
    assert batch == 1, "the prompt path handles one fresh stream"
    nh = d // HEAD_DIM
    depth = g_mix.shape[0]
    attn_tk, attn_chunk = 256, 512
    xp = x_prompt.reshape(seq, d)
    xs = x_sample.reshape(dec_batch * dec_seq, d)
    conv_p, conv_s, k_p, v_p, k_s, v_s = [], [], [], [], [], []
    w_down = w_down.astype(_BF16)
    for i in range(depth):
        j = i // N_MIXERS
        hp = _rmsnorm(xp, g_mix[i])
        hs = _rmsnorm(xs, g_mix[i])
        if i % N_MIXERS == 0:
            gp, tail_p = _conv_in(hp, w_conv_in, j, w_conv[j], jnp.zeros((CONV_W - 1, d), _F32), None)
            st = state_conv[j]
            inject = jnp.zeros((2, dec_batch, dec_seq, d), _F32)
            inject = inject.at[0, :, 0].set(st[:, 1]).at[1, :, 0].set(st[:, 0]).at[1, :, 1].set(st[:, 1])
            gs, u_s = _conv_in(hs, w_conv_in, j, w_conv[j], inject.reshape(2, dec_batch * dec_seq, d), dec_seq)
            conv_p.append(tail_p[_SUBLANES - (CONV_W - 1):].reshape(1, CONV_W - 1, d))
            conv_s.append(u_s.reshape(dec_batch, dec_seq, d)[:, dec_seq - (CONV_W - 1):])
            xp = _matmul_res(gp, w_conv_out, j, xp, 1024, 512)
            xs = _matmul_res(gs, w_conv_out, j, xs, 1024, 512)
        else:
            qp, kp, vtp, kfp, vfp = _qkv(hp, w_qkv, j, g_q[j], g_k[j], attn_chunk)
            qs, ks, vs, kfs, vfs = _qkv(hs, w_qkv, j, g_q[j], g_k[j], None)
            op = _attn_prompt(qp, kp, vtp, attn_tk, heads=4)
            os_ = _attn_sample(qs, ks, vs, cache_k, cache_v, j, attn_tk, attn_chunk, span=2048)
            k_p.append(kfp.reshape(batch, seq, nh, HEAD_DIM))
            v_p.append(vfp.reshape(batch, seq, nh, HEAD_DIM))
            k_s.append(kfs.reshape(dec_batch, dec_seq, nh, HEAD_DIM))
            v_s.append(vfs.reshape(dec_batch, dec_seq, nh, HEAD_DIM))
            xp = _matmul_res(op, w_o, j, xp, 1024, 512)
            xs = _matmul_res(os_, w_o, j, xs, 1024, 512)
        xp = _ffn(xp, g_ffn[i], w_gate, w_up, w_down, i)
        xs = _ffn(xs, g_ffn[i], w_gate, w_up, w_down, i)
    return (xp.reshape(batch, seq, d), xs.reshape(dec_batch, dec_seq, d),
            jnp.stack(conv_p), jnp.stack(conv_s), jnp.stack(k_p), jnp.stack(v_p), jnp.stack(k_s), jnp.stack(v_s))
```

```python
import functools

import jax
import jax.numpy as jnp
from jax import lax
from jax.experimental import pallas as pl
from jax.experimental.pallas import tpu as pltpu

EPS = 1e-6
HEAD_DIM = 128
CONV_W = 3
N_MIXERS = 2
SB_SCALE = HEAD_DIM ** -0.5
_LOG2E = 1.4426950408889634
_SIGN_BIT = -2 ** 31
_DEAD_LOG2 = -160.0

_BF16 = jnp.bfloat16
_F32 = jnp.float32
_SUBLANES = 8
_V7X_VMEM_BYTES = 64 * 1024 * 1024
_VMEM_CAP_BYTES = _V7X_VMEM_BYTES - 6 * 1024 * 1024


def _nbytes(shape, dtype):
    n = jnp.dtype(dtype).itemsize
    for s in shape:
        n *= s
    return n


def _params(n_grid, block_bytes, extra_bytes=0, resident_bytes=0):
    est = 2 * block_bytes + resident_bytes + extra_bytes + 4 * 1024 * 1024
    return pltpu.CompilerParams(
        dimension_semantics=("arbitrary",) * n_grid,
        vmem_limit_bytes=int(min(max(est, 16 * 1024 * 1024), _VMEM_CAP_BYTES)),
    )


def _tile(dim, want):
    t = min(dim, want)
    while dim % t:
        t -= 1
    return t


def _dot(a, b):
    return jnp.dot(a, b, preferred_element_type=_F32)


_RESIDENT = dict(pipeline_mode=pl.Buffered(1))


def _weight_tiling(m, tm, d):
    if m // tm > 1:
        return _tile(d, 512), _RESIDENT, 1
    return _tile(d, 256), {}, 2


def _cast_on_first(first, w_refs, scr_refs):
    @pl.when(first)
    def _():
        for w_ref, scr in zip(w_refs, scr_refs):
            scr[...] = w_ref[...].astype(scr.dtype)


def _rmsnorm_kernel(x_ref, g_ref, o_ref):
    x = x_ref[...]
    ms = jnp.mean(x * x, axis=-1, keepdims=True)
    o_ref[...] = (x * lax.rsqrt(ms + EPS) * g_ref[...]).astype(o_ref.dtype)


def _rmsnorm(x, g):
    m, d = x.shape
    tm = _tile(m, 256)
    blocks = _nbytes((tm, d), _F32) + _nbytes((tm, d), _BF16)
    return pl.pallas_call(
        _rmsnorm_kernel,
        grid=(m // tm,),
        in_specs=[pl.BlockSpec((tm, d), lambda i: (i, 0)), pl.BlockSpec((1, d), lambda i: (0, 0))],
        out_specs=pl.BlockSpec((tm, d), lambda i: (i, 0)),
        out_shape=jax.ShapeDtypeStruct((m, d), _BF16),
        compiler_params=_params(1, blocks, _nbytes((tm, d), _F32)),
        name="rmsnorm",
    )(x, g.reshape(1, d))


def _conv_in_kernel(h_ref, wb_ref, wc_ref, wx_ref, wconv_ref, st_ref, g_ref, tail_ref, u_scr,
                    wb_scr, wc_scr, wx_scr, *, seg_len, tail_rows):
    i = pl.program_id(1)
    _cast_on_first(i == 0, (wb_ref, wc_ref, wx_ref), (wb_scr, wc_scr, wx_scr))
    hb = h_ref[...]
    b = _dot(hb, wb_scr[...])
    c = _dot(hb, wc_scr[...])
    xin = _dot(hb, wx_scr[...])
    u = c * xin
    tm = u.shape[0]
    u_scr[_SUBLANES:_SUBLANES + tm, :] = u
    if seg_len is None:
        @pl.when(i == 0)
        def _():
            u_scr[0:_SUBLANES - 2, :] = jnp.zeros((_SUBLANES - 2, u.shape[1]), _F32)
            u_scr[_SUBLANES - 2:_SUBLANES, :] = st_ref[...]
        u1 = u_scr[_SUBLANES - 1:_SUBLANES - 1 + tm, :]
        u2 = u_scr[_SUBLANES - 2:_SUBLANES - 2 + tm, :]
    else:
        u_scr[0:_SUBLANES, :] = jnp.zeros((_SUBLANES, u.shape[1]), _F32)
        r = lax.broadcasted_iota(jnp.int32, u.shape, 0) % seg_len
        u1 = jnp.where(r == 0, st_ref[0], u_scr[_SUBLANES - 1:_SUBLANES - 1 + tm, :])
        u2 = jnp.where(r < 2, st_ref[1], u_scr[_SUBLANES - 2:_SUBLANES - 2 + tm, :])
    conv = u2 * wconv_ref[0:1, :] + u1 * wconv_ref[1:2, :] + u * wconv_ref[2:3, :]
    g_ref[...] = (b * conv).astype(g_ref.dtype)
    tail_ref[...] = u[tm - tail_rows:, :]
    if seg_len is None:
        u_scr[0:_SUBLANES, :] = u[tm - _SUBLANES:, :]


def _conv_in(h, w_in, layer, w_conv, state, seg_len):
    m, d = h.shape
    tm = _tile(m, 512) if seg_len is None else m
    tn, w_mode, w_bufs = _weight_tiling(m, tm, d)
    nj = d // tn
    if seg_len is None:
        tail_rows, tail_shape = _SUBLANES, (_SUBLANES, d)
        st_spec = pl.BlockSpec((2, tn), lambda j, i: (0, j))
        tail_spec = pl.BlockSpec((_SUBLANES, tn), lambda j, i: (0, j))
        st_bytes = _nbytes((2, tn), _F32)
    else:
        tail_rows, tail_shape = m, (m, d)
        st_spec = pl.BlockSpec((2, tm, tn), lambda j, i: (0, 0, j))
        tail_spec = pl.BlockSpec((tm, tn), lambda j, i: (0, j))
        st_bytes = _nbytes((2, tm, tn), _F32)
    blocks = _nbytes((tm, d), _BF16) + st_bytes + _nbytes((tm, tn), _BF16) + _nbytes((tail_rows, tn), _F32)
    scratch = _nbytes((tm + _SUBLANES, tn), _F32) + 3 * _nbytes((d, tn), _BF16)
    return pl.pallas_call(
        functools.partial(_conv_in_kernel, seg_len=seg_len, tail_rows=tail_rows),
        grid=(nj, m // tm),
        in_specs=[
            pl.BlockSpec((tm, d), lambda j, i: (i, 0)),
            pl.BlockSpec((None, d, tn), lambda j, i: (layer, 0, j), **w_mode),
            pl.BlockSpec((None, d, tn), lambda j, i: (layer, 0, j + nj), **w_mode),
            pl.BlockSpec((None, d, tn), lambda j, i: (layer, 0, j + 2 * nj), **w_mode),
            pl.BlockSpec((CONV_W, tn), lambda j, i: (0, j)),
            st_spec,
        ],
        out_specs=[pl.BlockSpec((tm, tn), lambda j, i: (i, j)), tail_spec],
        out_shape=[jax.ShapeDtypeStruct((m, d), _BF16), jax.ShapeDtypeStruct(tail_shape, _F32)],
        scratch_shapes=[pltpu.VMEM((tm + _SUBLANES, tn), _F32)] + [pltpu.VMEM((d, tn), _BF16)] * 3,
        compiler_params=_params(2, blocks, scratch + 6 * _nbytes((tm, tn), _F32),
                                3 * w_bufs * _nbytes((d, tn), _F32)),
        name="conv_in",
    )(h, w_in, w_in, w_in, w_conv, state)


def _matmul_res_kernel(a_ref, w_ref, r_ref, o_ref):
    o_ref[...] = _dot(a_ref[...], w_ref[...]) + r_ref[...]


def _matmul_res_resident_kernel(a_ref, w_ref, r_ref, o_ref, w_scr):
    _cast_on_first(pl.program_id(1) == 0, (w_ref,), (w_scr,))
    o_ref[...] = _dot(a_ref[...], w_scr[...]) + r_ref[...]


def _matmul_res(a, w, layer, res, tm_want, tn_want):
    m, k = a.shape
    n = w.shape[2]
    tm, tn = _tile(m, tm_want), _tile(n, tn_want)
    out_shape = jax.ShapeDtypeStruct((m, n), _F32)
    io_bytes = _nbytes((tm, k), _BF16) + 2 * _nbytes((tm, tn), _F32)
    if w.dtype == _F32:
        a_map, w_map, o_map = (lambda j, i: (i, 0)), (lambda j, i: (layer, 0, j)), (lambda j, i: (i, j))
        w_mode, w_bufs = (_RESIDENT, 1) if m // tm > 1 else ({}, 2)
        return pl.pallas_call(
            _matmul_res_resident_kernel,
            grid=(n // tn, m // tm),
            in_specs=[pl.BlockSpec((tm, k), a_map), pl.BlockSpec((None, k, tn), w_map, **w_mode),
                      pl.BlockSpec((tm, tn), o_map)],
            out_specs=pl.BlockSpec((tm, tn), o_map),
            out_shape=out_shape,
            scratch_shapes=[pltpu.VMEM((k, tn), _BF16)],
            compiler_params=_params(2, io_bytes, 2 * _nbytes((tm, tn), _F32) + _nbytes((k, tn), _BF16),
                                    w_bufs * _nbytes((k, tn), _F32)),
            name="matmul_res",
        )(a, w, res)
    a_map, w_map, o_map = (lambda i, j: (i, 0)), (lambda i, j: (layer, 0, j)), (lambda i, j: (i, j))
    return pl.pallas_call(
        _matmul_res_kernel,
        grid=(m // tm, n // tn),
        in_specs=[pl.BlockSpec((tm, k), a_map), pl.BlockSpec((None, k, tn), w_map), pl.BlockSpec((tm, tn), o_map)],
        out_specs=pl.BlockSpec((tm, tn), o_map),
        out_shape=out_shape,
        compiler_params=_params(2, io_bytes + _nbytes((k, tn), _BF16), 2 * _nbytes((tm, tn), _F32)),
        name="matmul_res",
    )(a, w, res)


def _ffn_up_kernel(h_ref, wg_ref, wu_ref, o_ref):
    hb = h_ref[...]
    gate = _dot(hb, wg_ref[...].astype(_BF16))
    up = _dot(hb, wu_ref[...].astype(_BF16))
    o_ref[...] = (gate * jax.nn.sigmoid(gate) * up).astype(o_ref.dtype)


def _ffn_up(h, wg, wu, layer):
    m, d = h.shape
    f = wg.shape[2]
    tm, tn = _tile(m, 2048), _tile(f, 256)
    blocks = _nbytes((tm, d), _BF16) + 2 * _nbytes((d, tn), _F32) + _nbytes((tm, tn), _BF16)
    return pl.pallas_call(
        _ffn_up_kernel,
        grid=(m // tm, f // tn),
        in_specs=[
            pl.BlockSpec((tm, d), lambda i, j: (i, 0)),
            pl.BlockSpec((None, d, tn), lambda i, j: (layer, 0, j)),
            pl.BlockSpec((None, d, tn), lambda i, j: (layer, 0, j)),
        ],
        out_specs=pl.BlockSpec((tm, tn), lambda i, j: (i, j)),
        out_shape=jax.ShapeDtypeStruct((m, f), _BF16),
        compiler_params=_params(2, blocks, 4 * _nbytes((tm, tn), _F32) + 2 * _nbytes((d, tn), _BF16)),
        name="ffn_up",
    )(h, wg, wu)


def _qkv_kernel(h_ref, wq_ref, wk_ref, wv_ref, gq_ref, gk_ref, qh_ref, kh_ref, vx_ref, kf_ref, vf_ref,
                wq_scr, wk_scr, wv_scr, *, key_chunk, q_scale):
    _cast_on_first(pl.program_id(1) == 0, (wq_ref, wk_ref, wv_ref), (wq_scr, wk_scr, wv_scr))
    hb = h_ref[...]
    q = _dot(hb, wq_scr[...])
    k = _dot(hb, wk_scr[...])
    v = _dot(hb, wv_scr[...])
    tm = q.shape[0]

    def head_norm(x, g):
        y = x * lax.rsqrt(jnp.mean(x * x, axis=-1, keepdims=True) + EPS)
        return y * g

    for hh in range(q.shape[1] // HEAD_DIM):
        sl = slice(hh * HEAD_DIM, (hh + 1) * HEAD_DIM)
        qh_ref[hh] = (head_norm(q[:, sl], gq_ref[...]) * q_scale).astype(qh_ref.dtype)
        kn = head_norm(k[:, sl], gk_ref[...])
        kf_ref[:, sl] = kn
        kh_ref[hh] = kn.astype(kh_ref.dtype)
        vh = v[:, sl]
        vf_ref[:, sl] = vh
        if key_chunk is None:
            vx_ref[hh] = vh.astype(vx_ref.dtype)
        else:
            vt = vh.T.astype(vx_ref.dtype)
            for cc in range(tm // key_chunk):
                vx_ref[hh, cc] = vt[:, cc * key_chunk:(cc + 1) * key_chunk]


def _qkv(h, w_qkv, layer, g_q, g_k, key_chunk):
    m, d = h.shape
    nh = d // HEAD_DIM
    tm = _tile(m, 512)
    tn, w_mode, w_bufs = _weight_tiling(m, tm, d)
    hpt = tn // HEAD_DIM
    nj = d // tn
    hm = jax.ShapeDtypeStruct((nh, m, HEAD_DIM), _BF16)
    hm_spec = pl.BlockSpec((hpt, tm, HEAD_DIM), lambda j, i: (j, i, 0))
    if key_chunk is None:
        vx, vx_spec = hm, hm_spec
    else:
        vx = jax.ShapeDtypeStruct((nh, m // key_chunk, HEAD_DIM, key_chunk), _BF16)
        vx_spec = pl.BlockSpec((hpt, tm // key_chunk, HEAD_DIM, key_chunk), lambda j, i: (j, i, 0, 0))
    blocks = _nbytes((tm, d), _BF16) + 3 * _nbytes((tm, tn), _BF16) + 2 * _nbytes((tm, tn), _F32)
    return pl.pallas_call(
        functools.partial(_qkv_kernel, key_chunk=key_chunk, q_scale=SB_SCALE * _LOG2E),
        grid=(nj, m // tm),
        in_specs=[
            pl.BlockSpec((tm, d), lambda j, i: (i, 0)),
            pl.BlockSpec((None, d, tn), lambda j, i: (layer, 0, j), **w_mode),
            pl.BlockSpec((None, d, tn), lambda j, i: (layer, 0, j + nj), **w_mode),
            pl.BlockSpec((None, d, tn), lambda j, i: (layer, 0, j + 2 * nj), **w_mode),
            pl.BlockSpec((1, HEAD_DIM), lambda j, i: (0, 0)),
            pl.BlockSpec((1, HEAD_DIM), lambda j, i: (0, 0)),
        ],
        out_specs=[hm_spec, hm_spec, vx_spec,
                   pl.BlockSpec((tm, tn), lambda j, i: (i, j)), pl.BlockSpec((tm, tn), lambda j, i: (i, j))],
        out_shape=[hm, hm, vx, jax.ShapeDtypeStruct((m, d), _F32), jax.ShapeDtypeStruct((m, d), _F32)],
        scratch_shapes=[pltpu.VMEM((d, tn), _BF16)] * 3,
        compiler_params=_params(2, blocks, 8 * _nbytes((tm, tn), _F32) + 3 * _nbytes((d, tn), _BF16),
                                3 * w_bufs * _nbytes((d, tn), _F32)),
        name="qkv",
    )(h, w_qkv, w_qkv, w_qkv, g_q.reshape(1, HEAD_DIM), g_k.reshape(1, HEAD_DIM))


def _sb_logits(kspan, q, tri, r_ref, masks):
    tk = tri.shape[0]
    z = lax.dot_general(kspan, q, (((1,), (1,)), ((), ())), preferred_element_type=_F32)
    neg_abs = lax.bitcast_convert_type(lax.bitcast_convert_type(z, jnp.int32) | _SIGN_BIT, _F32)
    sp = jnp.maximum(z, 0.0) + jnp.log(1.0 + jnp.exp2(neg_abs)) * _LOG2E
    r = r_ref[0:1, :]
    w_blocks = [None] * len(masks)
    for b in reversed(range(len(masks))):
        zb, spb = z[b * tk:(b + 1) * tk], sp[b * tk:(b + 1) * tk]
        if masks[b]:
            causal = lax.broadcasted_iota(jnp.int32, zb.shape, 0) < lax.broadcasted_iota(jnp.int32, zb.shape, 1)
            spb = jnp.where(causal, spb, 0.0)
        cs = _dot(tri, spb.astype(_BF16))
        wb = jnp.minimum(zb - cs, 0.0) + r
        w_blocks[b] = jnp.where(causal, wb, -jnp.inf) if masks[b] else wb
        r = r - cs[0:1, :]
    r_ref[0:1, :] = r
    return w_blocks[0] if len(masks) == 1 else jnp.concatenate(w_blocks, axis=0)


def _sb_alive(r_ref):
    return jnp.max(r_ref[:, 0, :]) >= _DEAD_LOG2


def _sb_accumulate(w, vt, acc_ref):
    acc_ref[...] += _dot(vt, jnp.exp2(w).astype(_BF16))


def _attn_prompt_kernel(q_ref, k_ref, vt_ref, tri_ref, o_ref, r_ref, acc_ref, w_ref, *, tk, heads, nsub):
    qi = pl.program_id(1)
    tri = tri_ref[...]
    chunk = nsub * tk
    chains = [(h, c) for h in range(heads) for c in range(nsub)]
    r_ref[...] = jnp.zeros_like(r_ref)
    acc_ref[...] = jnp.zeros_like(acc_ref)
    qs = {(h, c): q_ref[h, c * tk:(c + 1) * tk, :] for h, c in chains}

    dstart = pl.multiple_of(qi * chunk, chunk)
    for n, (h, c) in enumerate(chains):
        w = _sb_logits(k_ref[h, pl.ds(dstart, (c + 1) * tk), :], qs[h, c], tri, r_ref.at[n],
                       (False,) * c + (True,))
        if c + 1 < nsub:
            w = jnp.concatenate([w, jnp.full(((nsub - c - 1) * tk, tk), -jnp.inf, _F32)], axis=0)
        w_ref[n] = w

    def logits(kc):
        start = pl.multiple_of(kc * chunk, chunk)
        for n, (h, c) in enumerate(chains):
            w_ref[n] = _sb_logits(k_ref[h, pl.ds(start, chunk), :], qs[h, c], tri, r_ref.at[n], (False,) * nsub)

    def accumulate(kc):
        for n, (h, c) in enumerate(chains):
            _sb_accumulate(w_ref[n], vt_ref[h, kc], acc_ref.at[n])

    def body(carry):
        t, _ = carry
        accumulate(qi - t)
        logits(qi - 1 - t)
        return t + 1, _sb_alive(r_ref)

    t, _ = lax.while_loop(lambda carry: (carry[0] < qi) & carry[1], body, (jnp.int32(0), _sb_alive(r_ref)))
    accumulate(qi - t)
    for n, (h, c) in enumerate(chains):
        o_ref[c * tk:(c + 1) * tk, h * HEAD_DIM:(h + 1) * HEAD_DIM] = acc_ref[n].T.astype(o_ref.dtype)


def _tri(tk):
    idx = jnp.arange(tk)
    return (idx[None, :] >= idx[:, None]).astype(_BF16)


def _attn_prompt(q_hm, k_hm, vt, tk, heads):
    nh, m, hd = q_hm.shape
    chunk = vt.shape[3]
    nsub = chunk // tk
    blocks = 2 * _nbytes((heads, chunk, hd), _BF16) + _nbytes((tk, tk), _BF16)
    return pl.pallas_call(
        functools.partial(_attn_prompt_kernel, tk=tk, heads=heads, nsub=nsub),
        grid=(nh // heads, m // chunk),
        in_specs=[
            pl.BlockSpec((heads, chunk, hd), lambda g, i: (g, i, 0)),
            pl.BlockSpec((heads, m, hd), lambda g, i: (g, 0, 0), **_RESIDENT),
            pl.BlockSpec((heads, m // chunk, hd, chunk), lambda g, i: (g, 0, 0, 0), **_RESIDENT),
            pl.BlockSpec((tk, tk), lambda g, i: (0, 0)),
        ],
        out_specs=pl.BlockSpec((chunk, heads * hd), lambda g, i: (i, g)),
        out_shape=jax.ShapeDtypeStruct((m, nh * hd), _BF16),
        scratch_shapes=[pltpu.VMEM((heads * nsub, _SUBLANES, tk), _F32), pltpu.VMEM((heads * nsub, hd, tk), _F32),
                        pltpu.VMEM((heads * nsub, chunk, tk), _F32)],
        compiler_params=_params(2, blocks, 17 * heads * nsub * _nbytes((chunk, tk), _F32),
                                2 * _nbytes((heads, m, hd), _BF16)),
        name="sb_attn_prompt",
    )(q_hm, k_hm, vt, _tri(tk))


def _attn_sample_kernel(q_ref, kn_ref, vn_ref, ck_ref, cv_ref, tri_ref, o_ref, r_ref, acc_ref, w_ref,
                        *, tk, tq, nsub):
    step = pl.program_id(2)
    heads, t, hd = q_ref.shape
    chunk = nsub * tk
    n_chunks = ck_ref.shape[0] // chunk
    tri = tri_ref[...]
    qs = [jnp.concatenate([q_ref[h], jnp.zeros((tq - t, hd), _BF16)], axis=0) for h in range(heads)]

    @pl.when(step == 0)
    def _():
        r_ref[...] = jnp.zeros_like(r_ref)
        acc_ref[...] = jnp.zeros_like(acc_ref)
        for h in range(heads):
            k_new = jnp.concatenate([kn_ref[h], jnp.zeros((tk - t, hd), _BF16)], axis=0)
            v_new = jnp.concatenate([vn_ref[h].astype(_F32), jnp.zeros((tk - t, hd), _F32)], axis=0)
            w = _sb_logits(k_new, qs[h], tri, r_ref.at[h], (True,))
            _sb_accumulate(w, v_new.T.astype(_BF16), acc_ref.at[h])

    def logits(kc):
        start = pl.multiple_of(kc * chunk, chunk)
        for h in range(heads):
            kspan = ck_ref[pl.ds(start, chunk), h, :].astype(_BF16)
            w_ref[h] = _sb_logits(kspan, qs[h], tri, r_ref.at[h], (False,) * nsub)

    def accumulate(kc):
        start = pl.multiple_of(kc * chunk, chunk)
        for h in range(heads):
            vt = cv_ref[pl.ds(start, chunk), h, :].T.astype(_BF16)
            _sb_accumulate(w_ref[h], vt, acc_ref.at[h])

    @pl.when(_sb_alive(r_ref))
    def _():
        logits(n_chunks - 1)

        def body(carry):
            i, _ = carry
            accumulate(n_chunks - 1 - i)
            logits(n_chunks - 2 - i)
            return i + 1, _sb_alive(r_ref)

        i, _ = lax.while_loop(lambda carry: (carry[0] < n_chunks - 1) & carry[1], body,
                              (jnp.int32(0), _sb_alive(r_ref)))
        accumulate(n_chunks - 1 - i)

    @pl.when(step == pl.num_programs(2) - 1)
    def _():
        for h in range(heads):
            o_ref[:, h * hd:(h + 1) * hd] = acc_ref[h].T[0:t, :].astype(o_ref.dtype)


def _attn_sample(q_hm, k_hm, v_hm, cache_k, cache_v, layer, tk, chunk, span):
    nh, m, hd = q_hm.shape
    n_streams, past = cache_k.shape[1], cache_k.shape[2]
    t = m // n_streams
    heads = _SUBLANES
    assert past % span == 0 and span % chunk == 0 and nh % heads == 0 and t <= HEAD_DIM
    tq = HEAD_DIM
    n_spans = past // span
    grouped = (cache_k.shape[0], n_streams, past, nh // heads, heads, hd)
    new_spec = pl.BlockSpec((heads, t, hd), lambda s, g, c: (g, s, 0))
    cache_spec = pl.BlockSpec((None, None, span, None, heads, hd),
                              lambda s, g, c: (layer, s, n_spans - 1 - c, g, 0, 0))
    blocks = (3 * _nbytes((heads, t, hd), _BF16) + 2 * _nbytes((span, heads, hd), _F32)
              + _nbytes((tk, tk), _BF16))
    return pl.pallas_call(
        functools.partial(_attn_sample_kernel, tk=tk, tq=tq, nsub=chunk // tk),
        grid=(n_streams, nh // heads, n_spans),
        in_specs=[new_spec, new_spec, new_spec, cache_spec, cache_spec,
                  pl.BlockSpec((tk, tk), lambda s, g, c: (0, 0))],
        out_specs=pl.BlockSpec((t, heads * hd), lambda s, g, c: (s, g)),
        out_shape=jax.ShapeDtypeStruct((m, nh * hd), _BF16),
        scratch_shapes=[pltpu.VMEM((heads, _SUBLANES, tq), _F32), pltpu.VMEM((heads, hd, tq), _F32),
                        pltpu.VMEM((heads, chunk, tq), _F32)],
        compiler_params=_params(3, blocks, 17 * heads * _nbytes((chunk, tq), _F32)),
        name="sb_attn_sample",
    )(q_hm, k_hm, v_hm, cache_k.reshape(grouped), cache_v.reshape(grouped), _tri(tk))


def _ffn(x, g, wg, wu, wd, layer):
    mid = _ffn_up(_rmsnorm(x, g), wg, wu, layer)
    return _matmul_res(mid, wd, layer, x, 512, 512)


def kernel(x_prompt, x_sample, state_conv, cache_k, cache_v, g_mix, g_ffn, w_conv_in, w_conv, w_conv_out,
           w_qkv, g_q, g_k, w_o, w_gate, w_up, w_down):
    batch, seq, d = x_prompt.shape
    dec_batch, dec_seq, _ = x_sample.shape
    assert batch == 1, "the prompt path handles one fresh stream"
    nh = d // HEAD_DIM
    depth = g_mix.shape[0]
    attn_tk, attn_chunk = 256, 256
    xp = x_prompt.reshape(seq, d)
    xs = x_sample.reshape(dec_batch * dec_seq, d)
    conv_p, conv_s, k_p, v_p, k_s, v_s = [], [], [], [], [], []
    w_down = w_down.astype(_BF16)
    for i in range(depth):
        j = i // N_MIXERS
        hp = _rmsnorm(xp, g_mix[i])
        hs = _rmsnorm(xs, g_mix[i])
        if i % N_MIXERS == 0:
            gp, tail_p = _conv_in(hp, w_conv_in, j, w_conv[j], jnp.zeros((CONV_W - 1, d), _F32), None)
            st = state_conv[j]
            inject = jnp.zeros((2, dec_batch, dec_seq, d), _F32)
            inject = inject.at[0, :, 0].set(st[:, 1]).at[1, :, 0].set(st[:, 0]).at[1, :, 1].set(st[:, 1])
            gs, u_s = _conv_in(hs, w_conv_in, j, w_conv[j], inject.reshape(2, dec_batch * dec_seq, d), dec_seq)
            conv_p.append(tail_p[_SUBLANES - (CONV_W - 1):].reshape(1, CONV_W - 1, d))
            conv_s.append(u_s.reshape(dec_batch, dec_seq, d)[:, dec_seq - (CONV_W - 1):])
            xp = _matmul_res(gp, w_conv_out, j, xp, 1024, 512)
            xs = _matmul_res(gs, w_conv_out, j, xs, 1024, 512)
        else:
            qp, kp, vtp, kfp, vfp = _qkv(hp, w_qkv, j, g_q[j], g_k[j], attn_chunk)
            qs, ks, vs, kfs, vfs = _qkv(hs, w_qkv, j, g_q[j], g_k[j], None)
            op = _attn_prompt(qp, kp, vtp, attn_tk, heads=4)
            os_ = _attn_sample(qs, ks, vs, cache_k, cache_v, j, attn_tk, attn_chunk, span=2048)
            k_p.append(kfp.reshape(batch, seq, nh, HEAD_DIM))
            v_p.append(vfp.reshape(batch, seq, nh, HEAD_DIM))
            k_s.append(kfs.reshape(dec_batch, dec_seq, nh, HEAD_DIM))
            v_s.append(vfs.reshape(dec_batch, dec_seq, nh, HEAD_DIM))
            xp = _matmul_res(op, w_o, j, xp, 1024, 512)
            xs = _matmul_res(os_, w_o, j, xs, 1024, 512)
        xp = _ffn(xp, g_ffn[i], w_gate, w_up, w_down, i)
        xs = _ffn(xs, g_ffn[i], w_gate, w_up, w_down, i)
    return (xp.reshape(batch, seq, d), xs.reshape(dec_batch, dec_seq, d),
            jnp.stack(conv_p), jnp.stack(conv_s), jnp.stack(k_p), jnp.stack(v_p), jnp.stack(k_s), jnp.stack(v_s))
```

```python
import functools

import jax
import jax.numpy as jnp
from jax import lax
from jax.experimental import pallas as pl
from jax.experimental.pallas import tpu as pltpu

EPS = 1e-6
HEAD_DIM = 128
CONV_W = 3
N_MIXERS = 2
SB_SCALE = HEAD_DIM ** -0.5
_LOG2E = 1.4426950408889634
_SIGN_BIT = -2 ** 31
_DEAD_LOG2 = -float("inf")

_BF16 = jnp.bfloat16
_F32 = jnp.float32
_SUBLANES = 8
_V7X_VMEM_BYTES = 64 * 1024 * 1024
_VMEM_CAP_BYTES = _V7X_VMEM_BYTES - 6 * 1024 * 1024


def _nbytes(shape, dtype):
    n = jnp.dtype(dtype).itemsize
    for s in shape:
        n *= s
    return n


def _params(n_grid, block_bytes, extra_bytes=0, resident_bytes=0):
    est = 2 * block_bytes + resident_bytes + extra_bytes + 4 * 1024 * 1024
    return pltpu.CompilerParams(
        dimension_semantics=("arbitrary",) * n_grid,
        vmem_limit_bytes=int(min(max(est, 16 * 1024 * 1024), _VMEM_CAP_BYTES)),
    )


def _tile(dim, want):
    t = min(dim, want)
    while dim % t:
        t -= 1
    return t


def _dot(a, b):
    return jnp.dot(a, b, preferred_element_type=_F32)


_RESIDENT = dict(pipeline_mode=pl.Buffered(1))


def _weight_tiling(m, tm, d):
    if m // tm > 1:
        return _tile(d, 512), _RESIDENT, 1
    return _tile(d, 256), {}, 2


def _cast_on_first(first, w_refs, scr_refs):
    @pl.when(first)
    def _():
        for w_ref, scr in zip(w_refs, scr_refs):
            scr[...] = w_ref[...].astype(scr.dtype)


def _rmsnorm_kernel(x_ref, g_ref, o_ref):
    x = x_ref[...]
    ms = jnp.mean(x * x, axis=-1, keepdims=True)
    o_ref[...] = (x * lax.rsqrt(ms + EPS) * g_ref[...]).astype(o_ref.dtype)


def _rmsnorm(x, g):
    m, d = x.shape
    tm = _tile(m, 256)
    blocks = _nbytes((tm, d), _F32) + _nbytes((tm, d), _BF16)
    return pl.pallas_call(
        _rmsnorm_kernel,
        grid=(m // tm,),
        in_specs=[pl.BlockSpec((tm, d), lambda i: (i, 0)), pl.BlockSpec((1, d), lambda i: (0, 0))],
        out_specs=pl.BlockSpec((tm, d), lambda i: (i, 0)),
        out_shape=jax.ShapeDtypeStruct((m, d), _BF16),
        compiler_params=_params(1, blocks, _nbytes((tm, d), _F32)),
        name="rmsnorm",
    )(x, g.reshape(1, d))


def _conv_in_kernel(h_ref, wb_ref, wc_ref, wx_ref, wconv_ref, st_ref, g_ref, tail_ref, u_scr,
                    wb_scr, wc_scr, wx_scr, *, seg_len, tail_rows):
    i = pl.program_id(1)
    _cast_on_first(i == 0, (wb_ref, wc_ref, wx_ref), (wb_scr, wc_scr, wx_scr))
    hb = h_ref[...]
    b = _dot(hb, wb_scr[...])
    c = _dot(hb, wc_scr[...])
    xin = _dot(hb, wx_scr[...])
    u = c * xin
    tm = u.shape[0]
    u_scr[_SUBLANES:_SUBLANES + tm, :] = u
    if seg_len is None:
        @pl.when(i == 0)
        def _():
            u_scr[0:_SUBLANES - 2, :] = jnp.zeros((_SUBLANES - 2, u.shape[1]), _F32)
            u_scr[_SUBLANES - 2:_SUBLANES, :] = st_ref[...]
        u1 = u_scr[_SUBLANES - 1:_SUBLANES - 1 + tm, :]
        u2 = u_scr[_SUBLANES - 2:_SUBLANES - 2 + tm, :]
    else:
        u_scr[0:_SUBLANES, :] = jnp.zeros((_SUBLANES, u.shape[1]), _F32)
        r = lax.broadcasted_iota(jnp.int32, u.shape, 0) % seg_len
        u1 = jnp.where(r == 0, st_ref[0], u_scr[_SUBLANES - 1:_SUBLANES - 1 + tm, :])
        u2 = jnp.where(r < 2, st_ref[1], u_scr[_SUBLANES - 2:_SUBLANES - 2 + tm, :])
    conv = u2 * wconv_ref[0:1, :] + u1 * wconv_ref[1:2, :] + u * wconv_ref[2:3, :]
    g_ref[...] = (b * conv).astype(g_ref.dtype)
    tail_ref[...] = u[tm - tail_rows:, :]
    if seg_len is None:
        u_scr[0:_SUBLANES, :] = u[tm - _SUBLANES:, :]


def _conv_in(h, w_in, layer, w_conv, state, seg_len):
    m, d = h.shape
    tm = _tile(m, 512) if seg_len is None else m
    tn, w_mode, w_bufs = _weight_tiling(m, tm, d)
    nj = d // tn
    if seg_len is None:
        tail_rows, tail_shape = _SUBLANES, (_SUBLANES, d)
        st_spec = pl.BlockSpec((2, tn), lambda j, i: (0, j))
        tail_spec = pl.BlockSpec((_SUBLANES, tn), lambda j, i: (0, j))
        st_bytes = _nbytes((2, tn), _F32)
    else:
        tail_rows, tail_shape = m, (m, d)
        st_spec = pl.BlockSpec((2, tm, tn), lambda j, i: (0, 0, j))
        tail_spec = pl.BlockSpec((tm, tn), lambda j, i: (0, j))
        st_bytes = _nbytes((2, tm, tn), _F32)
    blocks = _nbytes((tm, d), _BF16) + st_bytes + _nbytes((tm, tn), _BF16) + _nbytes((tail_rows, tn), _F32)
    scratch = _nbytes((tm + _SUBLANES, tn), _F32) + 3 * _nbytes((d, tn), _BF16)
    return pl.pallas_call(
        functools.partial(_conv_in_kernel, seg_len=seg_len, tail_rows=tail_rows),
        grid=(nj, m // tm),
        in_specs=[
            pl.BlockSpec((tm, d), lambda j, i: (i, 0)),
            pl.BlockSpec((None, d, tn), lambda j, i: (layer, 0, j), **w_mode),
            pl.BlockSpec((None, d, tn), lambda j, i: (layer, 0, j + nj), **w_mode),
            pl.BlockSpec((None, d, tn), lambda j, i: (layer, 0, j + 2 * nj), **w_mode),
            pl.BlockSpec((CONV_W, tn), lambda j, i: (0, j)),
            st_spec,
        ],
        out_specs=[pl.BlockSpec((tm, tn), lambda j, i: (i, j)), tail_spec],
        out_shape=[jax.ShapeDtypeStruct((m, d), _BF16), jax.ShapeDtypeStruct(tail_shape, _F32)],
        scratch_shapes=[pltpu.VMEM((tm + _SUBLANES, tn), _F32)] + [pltpu.VMEM((d, tn), _BF16)] * 3,
        compiler_params=_params(2, blocks, scratch + 6 * _nbytes((tm, tn), _F32),
                                3 * w_bufs * _nbytes((d, tn), _F32)),
        name="conv_in",
    )(h, w_in, w_in, w_in, w_conv, state)


def _matmul_res_kernel(a_ref, w_ref, r_ref, o_ref):
    o_ref[...] = _dot(a_ref[...], w_ref[...]) + r_ref[...]


def _matmul_res_resident_kernel(a_ref, w_ref, r_ref, o_ref, w_scr):
    _cast_on_first(pl.program_id(1) == 0, (w_ref,), (w_scr,))
    o_ref[...] = _dot(a_ref[...], w_scr[...]) + r_ref[...]


def _matmul_res(a, w, layer, res, tm_want, tn_want):
    m, k = a.shape
    n = w.shape[2]
    tm, tn = _tile(m, tm_want), _tile(n, tn_want)
    out_shape = jax.ShapeDtypeStruct((m, n), _F32)
    io_bytes = _nbytes((tm, k), _BF16) + 2 * _nbytes((tm, tn), _F32)
    if w.dtype == _F32:
        a_map, w_map, o_map = (lambda j, i: (i, 0)), (lambda j, i: (layer, 0, j)), (lambda j, i: (i, j))
        w_mode, w_bufs = (_RESIDENT, 1) if m // tm > 1 else ({}, 2)
        return pl.pallas_call(
            _matmul_res_resident_kernel,
            grid=(n // tn, m // tm),
            in_specs=[pl.BlockSpec((tm, k), a_map), pl.BlockSpec((None, k, tn), w_map, **w_mode),
                      pl.BlockSpec((tm, tn), o_map)],
            out_specs=pl.BlockSpec((tm, tn), o_map),
            out_shape=out_shape,
            scratch_shapes=[pltpu.VMEM((k, tn), _BF16)],
            compiler_params=_params(2, io_bytes, 2 * _nbytes((tm, tn), _F32) + _nbytes((k, tn), _BF16),
                                    w_bufs * _nbytes((k, tn), _F32)),
            name="matmul_res",
        )(a, w, res)
    a_map, w_map, o_map = (lambda i, j: (i, 0)), (lambda i, j: (layer, 0, j)), (lambda i, j: (i, j))
    return pl.pallas_call(
        _matmul_res_kernel,
        grid=(m // tm, n // tn),
        in_specs=[pl.BlockSpec((tm, k), a_map), pl.BlockSpec((None, k, tn), w_map), pl.BlockSpec((tm, tn), o_map)],
        out_specs=pl.BlockSpec((tm, tn), o_map),
        out_shape=out_shape,
        compiler_params=_params(2, io_bytes + _nbytes((k, tn), _BF16), 2 * _nbytes((tm, tn), _F32)),
        name="matmul_res",
    )(a, w, res)


def _ffn_up_kernel(h_ref, wg_ref, wu_ref, o_ref):
    hb = h_ref[...]
    gate = _dot(hb, wg_ref[...].astype(_BF16))
    up = _dot(hb, wu_ref[...].astype(_BF16))
    o_ref[...] = (gate * jax.nn.sigmoid(gate) * up).astype(o_ref.dtype)


def _ffn_up(h, wg, wu, layer):
    m, d = h.shape
    f = wg.shape[2]
    tm, tn = _tile(m, 2048), _tile(f, 256)
    blocks = _nbytes((tm, d), _BF16) + 2 * _nbytes((d, tn), _F32) + _nbytes((tm, tn), _BF16)
    return pl.pallas_call(
        _ffn_up_kernel,
        grid=(m // tm, f // tn),
        in_specs=[
            pl.BlockSpec((tm, d), lambda i, j: (i, 0)),
            pl.BlockSpec((None, d, tn), lambda i, j: (layer, 0, j)),
            pl.BlockSpec((None, d, tn), lambda i, j: (layer, 0, j)),
        ],
        out_specs=pl.BlockSpec((tm, tn), lambda i, j: (i, j)),
        out_shape=jax.ShapeDtypeStruct((m, f), _BF16),
        compiler_params=_params(2, blocks, 4 * _nbytes((tm, tn), _F32) + 2 * _nbytes((d, tn), _BF16)),
        name="ffn_up",
    )(h, wg, wu)


def _qkv_kernel(h_ref, wq_ref, wk_ref, wv_ref, gq_ref, gk_ref, qh_ref, kh_ref, vx_ref, kf_ref, vf_ref,
                wq_scr, wk_scr, wv_scr, *, key_chunk, q_scale):
    _cast_on_first(pl.program_id(1) == 0, (wq_ref, wk_ref, wv_ref), (wq_scr, wk_scr, wv_scr))
    hb = h_ref[...]
    q = _dot(hb, wq_scr[...])
    k = _dot(hb, wk_scr[...])
    v = _dot(hb, wv_scr[...])
    tm = q.shape[0]

    def head_norm(x, g):
        y = x * lax.rsqrt(jnp.mean(x * x, axis=-1, keepdims=True) + EPS)
        return y * g

    for hh in range(q.shape[1] // HEAD_DIM):
        sl = slice(hh * HEAD_DIM, (hh + 1) * HEAD_DIM)
        qh_ref[hh] = (head_norm(q[:, sl], gq_ref[...]) * q_scale).astype(qh_ref.dtype)
        kn = head_norm(k[:, sl], gk_ref[...])
        kf_ref[:, sl] = kn
        kh_ref[hh] = kn.astype(kh_ref.dtype)
        vh = v[:, sl]
        vf_ref[:, sl] = vh
        if key_chunk is None:
            vx_ref[hh] = vh.astype(vx_ref.dtype)
        else:
            vt = vh.T.astype(vx_ref.dtype)
            for cc in range(tm // key_chunk):
                vx_ref[hh, cc] = vt[:, cc * key_chunk:(cc + 1) * key_chunk]


def _qkv(h, w_qkv, layer, g_q, g_k, key_chunk):
    m, d = h.shape
    nh = d // HEAD_DIM
    tm = _tile(m, 512)
    tn, w_mode, w_bufs = _weight_tiling(m, tm, d)
    hpt = tn // HEAD_DIM
    nj = d // tn
    hm = jax.ShapeDtypeStruct((nh, m, HEAD_DIM), _BF16)
    hm_spec = pl.BlockSpec((hpt, tm, HEAD_DIM), lambda j, i: (j, i, 0))
    if key_chunk is None:
        vx, vx_spec = hm, hm_spec
    else:
        vx = jax.ShapeDtypeStruct((nh, m // key_chunk, HEAD_DIM, key_chunk), _BF16)
        vx_spec = pl.BlockSpec((hpt, tm // key_chunk, HEAD_DIM, key_chunk), lambda j, i: (j, i, 0, 0))
    blocks = _nbytes((tm, d), _BF16) + 3 * _nbytes((tm, tn), _BF16) + 2 * _nbytes((tm, tn), _F32)
    return pl.pallas_call(
        functools.partial(_qkv_kernel, key_chunk=key_chunk, q_scale=SB_SCALE * _LOG2E),
        grid=(nj, m // tm),
        in_specs=[
            pl.BlockSpec((tm, d), lambda j, i: (i, 0)),
            pl.BlockSpec((None, d, tn), lambda j, i: (layer, 0, j), **w_mode),
            pl.BlockSpec((None, d, tn), lambda j, i: (layer, 0, j + nj), **w_mode),
            pl.BlockSpec((None, d, tn), lambda j, i: (layer, 0, j + 2 * nj), **w_mode),
            pl.BlockSpec((1, HEAD_DIM), lambda j, i: (0, 0)),
            pl.BlockSpec((1, HEAD_DIM), lambda j, i: (0, 0)),
        ],
        out_specs=[hm_spec, hm_spec, vx_spec,
                   pl.BlockSpec((tm, tn), lambda j, i: (i, j)), pl.BlockSpec((tm, tn), lambda j, i: (i, j))],
        out_shape=[hm, hm, vx, jax.ShapeDtypeStruct((m, d), _F32), jax.ShapeDtypeStruct((m, d), _F32)],
        scratch_shapes=[pltpu.VMEM((d, tn), _BF16)] * 3,
        compiler_params=_params(2, blocks, 8 * _nbytes((tm, tn), _F32) + 3 * _nbytes((d, tn), _BF16),
                                3 * w_bufs * _nbytes((d, tn), _F32)),
        name="qkv",
    )(h, w_qkv, w_qkv, w_qkv, g_q.reshape(1, HEAD_DIM), g_k.reshape(1, HEAD_DIM))


def _sb_logits(kspan, q, tri, r_ref, masks):
    tk = tri.shape[0]
    z = lax.dot_general(kspan, q, (((1,), (1,)), ((), ())), preferred_element_type=_F32)
    neg_abs = lax.bitcast_convert_type(lax.bitcast_convert_type(z, jnp.int32) | _SIGN_BIT, _F32)
    sp = jnp.maximum(z, 0.0) + jnp.log(1.0 + jnp.exp2(neg_abs)) * _LOG2E
    r = r_ref[0:1, :]
    w_blocks = [None] * len(masks)
    for b in reversed(range(len(masks))):
        zb, spb = z[b * tk:(b + 1) * tk], sp[b * tk:(b + 1) * tk]
        if masks[b]:
            causal = lax.broadcasted_iota(jnp.int32, zb.shape, 0) < lax.broadcasted_iota(jnp.int32, zb.shape, 1)
            spb = jnp.where(causal, spb, 0.0)
        cs = _dot(tri, spb.astype(_BF16))
        wb = jnp.minimum(zb - cs, 0.0) + r
        w_blocks[b] = jnp.where(causal, wb, -jnp.inf) if masks[b] else wb
        r = r - cs[0:1, :]
    r_ref[0:1, :] = r
    return w_blocks[0] if len(masks) == 1 else jnp.concatenate(w_blocks, axis=0)


def _sb_alive(r_ref):
    return jnp.max(r_ref[:, 0, :]) >= _DEAD_LOG2


def _sb_accumulate(w, vt, acc_ref):
    acc_ref[...] += _dot(vt, jnp.exp2(w).astype(_BF16))


def _attn_prompt_kernel(q_ref, k_ref, vt_ref, tri_ref, o_ref, r_ref, acc_ref, w_ref, *, tk, heads, nsub):
    qi = pl.program_id(1)
    tri = tri_ref[...]
    chunk = nsub * tk
    chains = [(h, c) for h in range(heads) for c in range(nsub)]
    r_ref[...] = jnp.zeros_like(r_ref)
    acc_ref[...] = jnp.zeros_like(acc_ref)
    qs = {(h, c): q_ref[h, c * tk:(c + 1) * tk, :] for h, c in chains}

    dstart = pl.multiple_of(qi * chunk, chunk)
    for n, (h, c) in enumerate(chains):
        w = _sb_logits(k_ref[h, pl.ds(dstart, (c + 1) * tk), :], qs[h, c], tri, r_ref.at[n],
                       (False,) * c + (True,))
        if c + 1 < nsub:
            w = jnp.concatenate([w, jnp.full(((nsub - c - 1) * tk, tk), -jnp.inf, _F32)], axis=0)
        w_ref[n] = w

    def logits(kc):
        start = pl.multiple_of(kc * chunk, chunk)
        for n, (h, c) in enumerate(chains):
            w_ref[n] = _sb_logits(k_ref[h, pl.ds(start, chunk), :], qs[h, c], tri, r_ref.at[n], (False,) * nsub)

    def accumulate(kc):
        for n, (h, c) in enumerate(chains):
            _sb_accumulate(w_ref[n], vt_ref[h, kc], acc_ref.at[n])

    def body(carry):
        t, _ = carry
        accumulate(qi - t)
        logits(qi - 1 - t)
        return t + 1, _sb_alive(r_ref)

    t, _ = lax.while_loop(lambda carry: (carry[0] < qi) & carry[1], body, (jnp.int32(0), _sb_alive(r_ref)))
    accumulate(qi - t)
    for n, (h, c) in enumerate(chains):
        o_ref[c * tk:(c + 1) * tk, h * HEAD_DIM:(h + 1) * HEAD_DIM] = acc_ref[n].T.astype(o_ref.dtype)


def _tri(tk):
    idx = jnp.arange(tk)
    return (idx[None, :] >= idx[:, None]).astype(_BF16)


def _attn_prompt(q_hm, k_hm, vt, tk, heads):
    nh, m, hd = q_hm.shape
    chunk = vt.shape[3]
    nsub = chunk // tk
    blocks = 2 * _nbytes((heads, chunk, hd), _BF16) + _nbytes((tk, tk), _BF16)
    return pl.pallas_call(
        functools.partial(_attn_prompt_kernel, tk=tk, heads=heads, nsub=nsub),
        grid=(nh // heads, m // chunk),
        in_specs=[
            pl.BlockSpec((heads, chunk, hd), lambda g, i: (g, i, 0)),
            pl.BlockSpec((heads, m, hd), lambda g, i: (g, 0, 0), **_RESIDENT),
            pl.BlockSpec((heads, m // chunk, hd, chunk), lambda g, i: (g, 0, 0, 0), **_RESIDENT),
            pl.BlockSpec((tk, tk), lambda g, i: (0, 0)),
        ],
        out_specs=pl.BlockSpec((chunk, heads * hd), lambda g, i: (i, g)),
        out_shape=jax.ShapeDtypeStruct((m, nh * hd), _BF16),
        scratch_shapes=[pltpu.VMEM((heads * nsub, _SUBLANES, tk), _F32), pltpu.VMEM((heads * nsub, hd, tk), _F32),
                        pltpu.VMEM((heads * nsub, chunk, tk), _F32)],
        compiler_params=_params(2, blocks, 17 * heads * nsub * _nbytes((chunk, tk), _F32),
                                2 * _nbytes((heads, m, hd), _BF16)),
        name="sb_attn_prompt",
    )(q_hm, k_hm, vt, _tri(tk))


def _attn_sample_kernel(q_ref, kn_ref, vn_ref, ck_ref, cv_ref, tri_ref, o_ref, r_ref, acc_ref, w_ref,
                        *, tk, tq, nsub):
    step = pl.program_id(2)
    heads, t, hd = q_ref.shape
    chunk = nsub * tk
    n_chunks = ck_ref.shape[0] // chunk
    tri = tri_ref[...]
    qs = [jnp.concatenate([q_ref[h], jnp.zeros((tq - t, hd), _BF16)], axis=0) for h in range(heads)]

    @pl.when(step == 0)
    def _():
        r_ref[...] = jnp.zeros_like(r_ref)
        acc_ref[...] = jnp.zeros_like(acc_ref)
        for h in range(heads):
            k_new = jnp.concatenate([kn_ref[h], jnp.zeros((tk - t, hd), _BF16)], axis=0)
            v_new = jnp.concatenate([vn_ref[h].astype(_F32), jnp.zeros((tk - t, hd), _F32)], axis=0)
            w = _sb_logits(k_new, qs[h], tri, r_ref.at[h], (True,))
            _sb_accumulate(w, v_new.T.astype(_BF16), acc_ref.at[h])

    def logits(kc):
        start = pl.multiple_of(kc * chunk, chunk)
        for h in range(heads):
            kspan = ck_ref[pl.ds(start, chunk), h, :].astype(_BF16)
            w_ref[h] = _sb_logits(kspan, qs[h], tri, r_ref.at[h], (False,) * nsub)

    def accumulate(kc):
        start = pl.multiple_of(kc * chunk, chunk)
        for h in range(heads):
            vt = cv_ref[pl.ds(start, chunk), h, :].T.astype(_BF16)
            _sb_accumulate(w_ref[h], vt, acc_ref.at[h])

    @pl.when(_sb_alive(r_ref))
    def _():
        logits(n_chunks - 1)

        def body(carry):
            i, _ = carry
            accumulate(n_chunks - 1 - i)
            logits(n_chunks - 2 - i)
            return i + 1, _sb_alive(r_ref)

        i, _ = lax.while_loop(lambda carry: (carry[0] < n_chunks - 1) & carry[1], body,
                              (jnp.int32(0), _sb_alive(r_ref)))
        accumulate(n_chunks - 1 - i)

    @pl.when(step == pl.num_programs(2) - 1)
    def _():
        for h in range(heads):
            o_ref[:, h * hd:(h + 1) * hd] = acc_ref[h].T[0:t, :].astype(o_ref.dtype)


def _attn_sample(q_hm, k_hm, v_hm, cache_k, cache_v, layer, tk, chunk, span):
    nh, m, hd = q_hm.shape
    n_streams, past = cache_k.shape[1], cache_k.shape[2]
    t = m // n_streams
    heads = _SUBLANES
    assert past % span == 0 and span % chunk == 0 and nh % heads == 0 and t <= HEAD_DIM
    tq = HEAD_DIM
    n_spans = past // span
    grouped = (cache_k.shape[0], n_streams, past, nh // heads, heads, hd)
    new_spec = pl.BlockSpec((heads, t, hd), lambda s, g, c: (g, s, 0))
    cache_spec = pl.BlockSpec((None, None, span, None, heads, hd),
                              lambda s, g, c: (layer, s, n_spans - 1 - c, g, 0, 0))
    blocks = (3 * _nbytes((heads, t, hd), _BF16) + 2 * _nbytes((span, heads, hd), _F32)
              + _nbytes((tk, tk), _BF16))
    return pl.pallas_call(
        functools.partial(_attn_sample_kernel, tk=tk, tq=tq, nsub=chunk // tk),
        grid=(n_streams, nh // heads, n_spans),
        in_specs=[new_spec, new_spec, new_spec, cache_spec, cache_spec,
                  pl.BlockSpec((tk, tk), lambda s, g, c: (0, 0))],
        out_specs=pl.BlockSpec((t, heads * hd), lambda s, g, c: (s, g)),
        out_shape=jax.ShapeDtypeStruct((m, nh * hd), _BF16),
        scratch_shapes=[pltpu.VMEM((heads, _SUBLANES, tq), _F32), pltpu.VMEM((heads, hd, tq), _F32),
                        pltpu.VMEM((heads, chunk, tq), _F32)],
        compiler_params=_params(3, blocks, 17 * heads * _nbytes((chunk, tq), _F32)),
        name="sb_attn_sample",
    )(q_hm, k_hm, v_hm, cache_k.reshape(grouped), cache_v.reshape(grouped), _tri(tk))


def _ffn(x, g, wg, wu, wd, layer):
    mid = _ffn_up(_rmsnorm(x, g), wg, wu, layer)
    return _matmul_res(mid, wd, layer, x, 512, 512)


def kernel(x_prompt, x_sample, state_conv, cache_k, cache_v, g_mix, g_ffn, w_conv_in, w_conv, w_conv_out,
           w_qkv, g_q, g_k, w_o, w_gate, w_up, w_down):
    batch, seq, d = x_prompt.shape
    dec_batch, dec_seq, _ = x_sample.shape
    assert batch == 1, "the prompt path handles one fresh stream"
    nh = d // HEAD_DIM
    depth = g_mix.shape[0]
    attn_tk, attn_chunk = 256, 512
    xp = x_prompt.reshape(seq, d)
    xs = x_sample.reshape(dec_batch * dec_seq, d)
    conv_p, conv_s, k_p, v_p, k_s, v_s = [], [], [], [], [], []
    w_down = w_down.astype(_BF16)
    for i in range(depth):
        j = i // N_MIXERS
        hp = _rmsnorm(xp, g_mix[i])
        hs = _rmsnorm(xs, g_mix[i])
        if i % N_MIXERS == 0:
            gp, tail_p = _conv_in(hp, w_conv_in, j, w_conv[j], jnp.zeros((CONV_W - 1, d), _F32), None)
            st = state_conv[j]
            inject = jnp.zeros((2, dec_batch, dec_seq, d), _F32)
            inject = inject.at[0, :, 0].set(st[:, 1]).at[1, :, 0].set(st[:, 0]).at[1, :, 1].set(st[:, 1])
            gs, u_s = _conv_in(hs, w_conv_in, j, w_conv[j], inject.reshape(2, dec_batch * dec_seq, d), dec_seq)
            conv_p.append(tail_p[_SUBLANES - (CONV_W - 1):].reshape(1, CONV_W - 1, d))
            conv_s.append(u_s.reshape(dec_batch, dec_seq, d)[:, dec_seq - (CONV_W - 1):])
            xp = _matmul_res(gp, w_conv_out, j, xp, 1024, 512)
            xs = _matmul_res(gs, w_conv_out, j, xs, 1024, 512)
        else:
            qp, kp, vtp, kfp, vfp = _qkv(hp, w_qkv, j, g_q[j], g_k[j], attn_chunk)
            qs, ks, vs, kfs, vfs = _qkv(hs, w_qkv, j, g_q[j], g_k[j], None)
            op = _attn_prompt(qp, kp, vtp, attn_tk, heads=4)
            os_ = _attn_sample(qs, ks, vs, cache_k, cache_v, j, attn_tk, attn_chunk, span=2048)
            k_p.append(kfp.reshape(batch, seq, nh, HEAD_DIM))
            v_p.append(vfp.reshape(batch, seq, nh, HEAD_DIM))
            k_s.append(kfs.reshape(dec_batch, dec_seq, nh, HEAD_DIM))
            v_s.append(vfs.reshape(dec_batch, dec_seq, nh, HEAD_DIM))
            xp = _matmul_res(op, w_o, j, xp, 1024, 512)
            xs = _matmul_res(os_, w_o, j, xs, 1024, 512)
        xp = _ffn(xp, g_ffn[i], w_gate, w_up, w_down, i)
        xs = _ffn(xs, g_ffn[i], w_gate, w_up, w_down, i)
    return (xp.reshape(batch, seq, d), xs.reshape(dec_batch, dec_seq, d),
            jnp.stack(conv_p), jnp.stack(conv_s), jnp.stack(k_p), jnp.stack(v_p), jnp.stack(k_s), jnp.stack(v_s))
```
